```python
import math, functools
import jax, jax.numpy as jnp
from jax import lax
import numpy as np

D_MODEL = 4096
BATCH = 2
SEQ = 4096
DEPTH = 2
DEC_BATCH = 8
DEC_SEQ = 2048
PAST_LEN = 128

N_META = 16
GRID_W = 64
MIX_WIDTH = D_MODEL
GROUP_WIDTH = MIX_WIDTH // 4
D_FF = 4 * D_MODEL
EPS = 1e-6
CHUNK = 128

SSD_HEAD_DIM = 64
SSD_HEADS = GROUP_WIDTH // SSD_HEAD_DIM
SSD_GROUPS = 2
SSD_HPG = SSD_HEADS // SSD_GROUPS
SSD_STATE = 128
SSD_CONV_W = 5
D_INNER = GROUP_WIDTH
SSD_CONV_DIM = D_INNER + 2 * SSD_GROUPS * SSD_STATE
SSD_COLS = D_INNER + SSD_CONV_DIM + 2 * SSD_HEADS

SWA_HEAD_DIM = 128
SWA_HEADS = GROUP_WIDTH // SWA_HEAD_DIM
SWA_KV_HEADS = 2
SWA_WINDOW = 128
SWA_BLOCK = 128
SWA_COLS = (SWA_HEADS + 2 * SWA_KV_HEADS) * SWA_HEAD_DIM

NA_HEAD_DIM = 64
NA_HEADS = GROUP_WIDTH // NA_HEAD_DIM
NA_KH = 8
NA_KW = 16
NA_COLS = 3 * NA_HEADS * NA_HEAD_DIM

RET_HEADS = 8
RET_V_DIM = GROUP_WIDTH // RET_HEADS
RET_K_DIM = RET_V_DIM // 2
RET_COLS = 2 * RET_HEADS * RET_K_DIM + 2 * GROUP_WIDTH

IN_COLS = SSD_COLS + SWA_COLS + NA_COLS + RET_COLS

kernel_name = 'hybrid_bidir_encoder'


def rms_norm(x, w):
    xf = x.astype(jnp.float32)
    y = xf * lax.rsqrt(jnp.mean(xf * xf, axis=-1, keepdims=True) + EPS)
    return (y * w.astype(jnp.float32)).astype(x.dtype)


def chunk_decay_scan(q, k, v, log_a, include_diag):
    b, p = q.shape[:2]
    nc = p // CHUNK
    qc = q.reshape(b, nc, CHUNK, *q.shape[2:])
    kc = k.reshape(b, nc, CHUNK, *k.shape[2:])
    vc = v.reshape(b, nc, CHUNK, *v.shape[2:])
    acum = jnp.cumsum(log_a.astype(jnp.float32).reshape(b, nc, CHUNK, *log_a.shape[2:]), axis=2)
    mask = np.tril(np.ones((CHUNK, CHUNK), dtype=bool), k=0 if include_diag else -1)
    seg = jnp.where(mask[None, None, :, :, None, None],
                    acum[:, :, :, None] - acum[:, :, None, :], -jnp.inf)
    scores = jnp.einsum('bclgn,bcsgn->bclsg', qc, kc)
    y_intra = jnp.einsum('bclsgh,bcsghp->bclghp', scores[..., None] * jnp.exp(seg), vc)
    to_end = jnp.exp(acum[:, :, -1:] - acum)
    states = jnp.einsum('bcsgn,bcsghp->bcghnp', kc, vc * to_end[..., None])
    chunk_decay = jnp.exp(acum[:, :, -1])

    def step(carry, inp):
        st, dec = inp
        return carry * dec[..., None, None] + st, carry

    _, prev = lax.scan(step, jnp.zeros_like(states[:, 0]),
                       (jnp.swapaxes(states, 0, 1), jnp.swapaxes(chunk_decay, 0, 1)))
    prev = jnp.swapaxes(prev, 0, 1)
    y_inter = jnp.einsum('bclgn,bcghnp->bclghp', qc, prev) * jnp.exp(acum)[..., None]
    return (y_intra + y_inter).reshape(v.shape).astype(v.dtype)


def bidir_decay_scan(q, k, v_f, v_b, la_f, la_b):
    pad = CHUNK - N_META

    def front(a):
        return jnp.pad(a, [(0, 0), (pad, 0)] + [(0, 0)] * (a.ndim - 2))

    def rev(a):
        return jnp.flip(a, axis=1)

    qp, kp = front(q), front(k)
    fwd = chunk_decay_scan(qp, kp, front(v_f), front(la_f), True)
    bwd = rev(chunk_decay_scan(rev(qp), rev(kp), rev(front(v_b)), rev(front(la_b)), False))
    return (fwd + bwd)[:, pad:]


def centred_depthwise_conv(x, w, b):
    half = (SSD_CONV_W - 1) // 2
    y = lax.conv_general_dilated(x, w[:, None, :].astype(x.dtype), window_strides=(1,),
                                 padding=[(half, half)], dimension_numbers=('NWC', 'WIO', 'NWC'),
                                 feature_group_count=x.shape[-1])
    return y + b.astype(x.dtype)


def ssd_mixer(p, conv_w, conv_b, dt_bias, a_log, d_skip, norm_w):
    bsz, L = p.shape[:2]
    z, xbc, dt = jnp.split(p, [D_INNER, D_INNER + SSD_CONV_DIM], axis=-1)
    xbc = jax.nn.silu(centred_depthwise_conv(xbc, conv_w, conv_b))
    xs, bm, cm = jnp.split(xbc, [D_INNER, D_INNER + SSD_GROUPS * SSD_STATE], axis=-1)
    xs = xs.reshape(bsz, L, SSD_GROUPS, SSD_HPG, SSD_HEAD_DIM)
    bm = bm.reshape(bsz, L, SSD_GROUPS, SSD_STATE)
    cm = cm.reshape(bsz, L, SSD_GROUPS, SSD_STATE)
    dt = jax.nn.softplus(dt.astype(jnp.float32) + dt_bias.reshape(-1).astype(jnp.float32))
    dt = dt.reshape(bsz, L, 2, SSD_GROUPS, SSD_HPG)
    la = dt * (-jnp.exp(a_log.astype(jnp.float32))).reshape(2, SSD_GROUPS, SSD_HPG)
    v = xs[:, :, None] * dt[..., None]
    y = bidir_decay_scan(cm, bm, v[:, :, 0], v[:, :, 1], la[:, :, 0], la[:, :, 1])
    y = y + xs * d_skip.reshape(SSD_GROUPS, SSD_HPG, 1)
    y = y.reshape(bsz, L, D_INNER).astype(jnp.float32) * jax.nn.silu(z.astype(jnp.float32))
    yg = y.reshape(bsz, L, SSD_GROUPS, D_INNER // SSD_GROUPS)
    yg = yg * lax.rsqrt(jnp.mean(yg * yg, axis=-1, keepdims=True) + EPS)
    return (yg.reshape(bsz, L, D_INNER) * norm_w.astype(jnp.float32)).astype(p.dtype)


def alibi_slopes(n):
    return (2.0 ** (-8.0 * np.arange(1, n + 1) / n)).astype(np.float32)


def swa_mixer(p, sink):
    bsz, L = p.shape[:2]
    t = L - N_META
    g, r, dh, blk = SWA_KV_HEADS, SWA_HEADS // SWA_KV_HEADS, SWA_HEAD_DIM, SWA_BLOCK
    nb = t // blk
    scale = dh ** -0.5
    q, k, v = jnp.split(p, [SWA_HEADS * dh, (SWA_HEADS + g) * dh], axis=-1)
    q = q.reshape(bsz, L, g, r, dh)
    k = k.reshape(bsz, L, g, dh)
    v = v.reshape(bsz, L, g, dh)
    qm, qr = q[:, :N_META], q[:, N_META:]
    km, kr = k[:, :N_META], k[:, N_META:]
    vm, vr = v[:, :N_META], v[:, N_META:]
    slopes = alibi_slopes(SWA_HEADS).reshape(g, r)
    sink = sink.astype(jnp.float32).reshape(g, r)

    qb = qr.reshape(bsz, nb, blk, g, r, dh)

    def band(a):
        ap = jnp.pad(a, ((0, 0), (blk, blk), (0, 0), (0, 0)))
        return jnp.concatenate([ap[:, j * blk: j * blk + t].reshape(bsz, nb, blk, g, dh) for j in range(3)], axis=2)

    kb, vb = band(kr), band(vr)
    rel = np.arange(3 * blk)[None, :] - blk - np.arange(blk)[:, None]
    dist = np.abs(rel).astype(np.float32)
    key_pos = np.arange(nb)[:, None] * blk - blk + np.arange(3 * blk)[None, :]
    valid = (np.abs(rel) <= SWA_WINDOW)[None] & ((key_pos >= 0) & (key_pos < t))[:, None, :]
    s_band = (jnp.einsum('bnqgrd,bnkgd->bngrqk', qb, kb).astype(jnp.float32) * scale
              - slopes[:, :, None, None] * dist)
    s_band = jnp.where(valid[None, :, None, None], s_band, -jnp.inf)
    s_meta = jnp.einsum('bnqgrd,bmgd->bngrqm', qb, km).astype(jnp.float32) * scale
    s_sink = jnp.broadcast_to(sink[None, None, :, :, None, None], s_meta.shape[:-1] + (1,))
    probs = jax.nn.softmax(jnp.concatenate([s_band, s_meta, s_sink], axis=-1), axis=-1)
    o_real = (jnp.einsum('bngrqk,bnkgd->bnqgrd', probs[..., :3 * blk], vb)
              + jnp.einsum('bngrqm,bmgd->bnqgrd', probs[..., 3 * blk:3 * blk + N_META], vm))
    o_real = o_real.reshape(bsz, t, SWA_HEADS * dh)

    k0, v0 = kr[:, :blk], vr[:, :blk]
    mdist = N_META + np.arange(blk)[None, :] - np.arange(N_META)[:, None]
    sm_real = (jnp.einsum('bqgrd,bkgd->bgrqk', qm, k0).astype(jnp.float32) * scale
               - slopes[:, :, None, None] * mdist.astype(np.float32))
    sm_real = jnp.where((mdist <= SWA_WINDOW)[None, None, None], sm_real, -jnp.inf)
    sm_meta = jnp.einsum('bqgrd,bmgd->bgrqm', qm, km).astype(jnp.float32) * scale
    sm_sink = jnp.broadcast_to(sink[None, :, :, None, None], sm_meta.shape[:-1] + (1,))
    pm = jax.nn.softmax(jnp.concatenate([sm_real, sm_meta, sm_sink], axis=-1), axis=-1)
    o_meta = (jnp.einsum('bgrqk,bkgd->bqgrd', pm[..., :blk], v0)
              + jnp.einsum('bgrqm,bmgd->bqgrd', pm[..., blk:blk + N_META], vm))
    o_meta = o_meta.reshape(bsz, N_META, SWA_HEADS * dh)
    return jnp.concatenate([o_meta, o_real], axis=1).astype(p.dtype)


def na_mixer(p, rpb):
    bsz, L = p.shape[:2]
    t = L - N_META
    h, dh = NA_HEADS, NA_HEAD_DIM
    rows = t // GRID_W
    kh = min(NA_KH, rows)
    scale = dh ** -0.5
    q, k, v = [a.reshape(bsz, L, h, dh) for a in jnp.split(p, 3, axis=-1)]
    qm, km, vm = q[:, :N_META], k[:, :N_META], v[:, :N_META]
    qg = q[:, N_META:].reshape(bsz, rows, GRID_W, h, dh)
    kg = k[:, N_META:].reshape(bsz, rows, GRID_W, h, dh)
    vg = v[:, N_META:].reshape(bsz, rows, GRID_W, h, dh)
    col_start = np.clip(np.arange(GRID_W) - NA_KW // 2, 0, GRID_W - NA_KW)
    col_idx = col_start[:, None] + np.arange(NA_KW)[None, :]
    col_off = col_idx - np.arange(GRID_W)[:, None] + NA_KW - 1
    rpb = rpb.astype(jnp.float32)

    def row_block(r):
        q_r = lax.dynamic_index_in_dim(qg, r, axis=1, keepdims=False)
        r0 = jnp.clip(r - kh // 2, 0, rows - kh)
        k_sel = lax.dynamic_slice_in_dim(kg, r0, kh, axis=1)[:, :, col_idx]
        v_sel = lax.dynamic_slice_in_dim(vg, r0, kh, axis=1)[:, :, col_idx]
        s = jnp.einsum('bchd,bkcwhd->bhckw', q_r, k_sel).astype(jnp.float32) * scale
        row_off = r0 + jnp.arange(kh) - r + NA_KH - 1
        bias = rpb[:, row_off[:, None, None], col_off[None, :, :]]
        s = s + jnp.transpose(bias, (0, 2, 1, 3))[None]
        s_m = jnp.einsum('bchd,bmhd->bhcm', q_r, km).astype(jnp.float32) * scale
        probs = jax.nn.softmax(jnp.concatenate([s.reshape(bsz, h, GRID_W, kh * NA_KW), s_m], axis=-1), axis=-1)
        pw = probs[..., :kh * NA_KW].reshape(bsz, h, GRID_W, kh, NA_KW)
        o = (jnp.einsum('bhckw,bkcwhd->bchd', pw, v_sel)
             + jnp.einsum('bhcm,bmhd->bchd', probs[..., kh * NA_KW:], vm))
        return o.astype(p.dtype)

    o_real = lax.map(row_block, jnp.arange(rows))
    o_real = jnp.moveaxis(o_real, 0, 1).reshape(bsz, t, h * dh)

    kw0 = kg[:, :kh, :NA_KW].reshape(bsz, kh * NA_KW, h, dh)
    vw0 = vg[:, :kh, :NA_KW].reshape(bsz, kh * NA_KW, h, dh)
    s_w = jnp.einsum('bqhd,bkhd->bhqk', qm, kw0).astype(jnp.float32) * scale
    s_mm = jnp.einsum('bqhd,bmhd->bhqm', qm, km).astype(jnp.float32) * scale
    pm = jax.nn.softmax(jnp.concatenate([s_w, s_mm], axis=-1), axis=-1)
    o_meta = (jnp.einsum('bhqk,bkhd->bqhd', pm[..., :kh * NA_KW], vw0)
              + jnp.einsum('bhqm,bmhd->bqhd', pm[..., kh * NA_KW:], vm))
    o_meta = o_meta.reshape(bsz, N_META, h * dh)
    return jnp.concatenate([o_meta, o_real], axis=1).astype(p.dtype)


def retention_mixer(p, decay_logit):
    bsz, L = p.shape[:2]
    qk = RET_HEADS * RET_K_DIM
    q, k, v, gate = jnp.split(p, [qk, 2 * qk, 2 * qk + GROUP_WIDTH], axis=-1)
    q = q.reshape(bsz, L, RET_HEADS, RET_K_DIM)
    k = k.reshape(bsz, L, RET_HEADS, RET_K_DIM) * (RET_K_DIM ** -0.5)
    v = v.reshape(bsz, L, RET_HEADS, 1, RET_V_DIM)
    log_gamma = jax.nn.log_sigmoid(decay_logit.astype(jnp.float32))
    la_f = jnp.broadcast_to(log_gamma[0][:, None], (bsz, L, RET_HEADS, 1))
    la_b = jnp.broadcast_to(log_gamma[1][:, None], (bsz, L, RET_HEADS, 1))
    y = bidir_decay_scan(q, k, v, v, la_f, la_b)[:, :, :, 0].astype(jnp.float32)
    mu = jnp.mean(y, axis=-1, keepdims=True)
    var = jnp.mean(jnp.square(y - mu), axis=-1, keepdims=True)
    y = (y - mu) * lax.rsqrt(var + 1e-5)
    return (y.reshape(bsz, L, GROUP_WIDTH) * jax.nn.silu(gate.astype(jnp.float32))).astype(p.dtype)


def trunk(x, meta_tokens, norm1_w, w_in, ssd_conv_w, ssd_conv_b, ssd_dt_bias, ssd_a_log, ssd_d,
          ssd_norm_w, swa_sink, na_rpb, ret_decay, w_out, norm2_w, w_up, w_down, final_norm_w):
    bsz = x.shape[0]
    meta = jnp.broadcast_to(meta_tokens.astype(x.dtype)[None], (bsz, N_META, x.shape[-1]))
    h = jnp.concatenate([meta, x], axis=1)
    splits = [SSD_COLS, SSD_COLS + SWA_COLS, SSD_COLS + SWA_COLS + NA_COLS]
    for i in range(DEPTH):
        n = rms_norm(h, norm1_w[i])
        p_ssd, p_swa, p_na, p_ret = jnp.split(n @ w_in[i], splits, axis=-1)
        mixed = jnp.concatenate([
            ssd_mixer(p_ssd, ssd_conv_w[i], ssd_conv_b[i], ssd_dt_bias[i], ssd_a_log[i], ssd_d[i], ssd_norm_w[i]),
            swa_mixer(p_swa, swa_sink[i]),
            na_mixer(p_na, na_rpb[i]),
            retention_mixer(p_ret, ret_decay[i]),
        ], axis=-1)
        h = h + mixed @ w_out[i]
        n2 = rms_norm(h, norm2_w[i])
        h = h + jnp.square(jax.nn.relu(n2 @ w_up[i])) @ w_down[i]
    return rms_norm(h, final_norm_w)[:, N_META:]


def setup_inputs(seed: int = 0) -> dict:
    key = jax.random.key(seed)
    ks = jax.random.split(key, 20)
    nrm = jax.random.normal
    d = D_MODEL
    dt0 = jnp.exp(jax.random.uniform(ks[6], (DEPTH, 2, SSD_HEADS), minval=math.log(1e-3), maxval=math.log(1e-1)))
    dt_bias = dt0 + jnp.log(-jnp.expm1(-dt0))
    ret_init = np.log(2.0 ** (5 + np.arange(RET_HEADS)) - 1.0).astype(np.float32)
    return {
        'x_prompt': nrm(ks[0], (BATCH, SEQ, d), jnp.float32),
        'x_sample': nrm(ks[1], (DEC_BATCH, DEC_SEQ, d), jnp.float32),
        'meta_tokens': nrm(ks[2], (N_META, d), jnp.float32),
        'norm1_w': 1.0 + 0.02 * nrm(ks[3], (DEPTH, d), jnp.float32),
        'w_in': nrm(ks[4], (DEPTH, d, IN_COLS), jnp.float32) * d ** -0.5,
        'ssd_conv_w': nrm(ks[5], (DEPTH, SSD_CONV_W, SSD_CONV_DIM), jnp.float32) * SSD_CONV_W ** -0.5,
        'ssd_conv_b': 0.02 * nrm(ks[7], (DEPTH, SSD_CONV_DIM), jnp.float32),
        'ssd_dt_bias': dt_bias,
        'ssd_a_log': jnp.log(jax.random.uniform(ks[8], (DEPTH, 2, SSD_HEADS), minval=1.0, maxval=16.0)),
        'ssd_d': 1.0 + 0.1 * nrm(ks[9], (DEPTH, SSD_HEADS), jnp.float32),
        'ssd_norm_w': 1.0 + 0.02 * nrm(ks[10], (DEPTH, D_INNER), jnp.float32),
        'swa_sink': 0.5 * nrm(ks[11], (DEPTH, SWA_HEADS), jnp.float32),
        'na_rpb': 0.1 * nrm(ks[12], (DEPTH, NA_HEADS, 2 * NA_KH - 1, 2 * NA_KW - 1), jnp.float32),
        'ret_decay': jnp.asarray(ret_init) + 0.1 * nrm(ks[13], (DEPTH, 2, RET_HEADS), jnp.float32),
        'w_out': nrm(ks[14], (DEPTH, MIX_WIDTH, d), jnp.float32) * MIX_WIDTH ** -0.5,
        'norm2_w': 1.0 + 0.02 * nrm(ks[15], (DEPTH, d), jnp.float32),
        'w_up': nrm(ks[16], (DEPTH, d, D_FF), jnp.float32) * d ** -0.5,
        'w_down': nrm(ks[17], (DEPTH, D_FF, d), jnp.float32) * D_FF ** -0.5,
        'final_norm_w': 1.0 + 0.02 * nrm(ks[18], (d,), jnp.float32),
    }


def reference(x_prompt, x_sample, meta_tokens, norm1_w, w_in, ssd_conv_w, ssd_conv_b, ssd_dt_bias,
              ssd_a_log, ssd_d, ssd_norm_w, swa_sink, na_rpb, ret_decay, w_out, norm2_w, w_up, w_down,
              final_norm_w):
    run = functools.partial(
        trunk, meta_tokens=meta_tokens, norm1_w=norm1_w, w_in=w_in, ssd_conv_w=ssd_conv_w,
        ssd_conv_b=ssd_conv_b, ssd_dt_bias=ssd_dt_bias, ssd_a_log=ssd_a_log, ssd_d=ssd_d,
        ssd_norm_w=ssd_norm_w, swa_sink=swa_sink, na_rpb=na_rpb, ret_decay=ret_decay, w_out=w_out,
        norm2_w=norm2_w, w_up=w_up, w_down=w_down, final_norm_w=final_norm_w)
    y_prompt = run(x_prompt)
    y_sample = run(x_sample)
    return (y_prompt, y_sample)
```

```python
import functools
import numpy as np
import jax
import jax.numpy as jnp
from jax import lax
from jax.experimental import pallas as pl
from jax.experimental.pallas import tpu as pltpu

F32 = jnp.float32
BF16 = jnp.bfloat16
HIGHEST = lax.Precision.HIGHEST

D_MODEL = 4096
DEPTH = 2
N_META = 16
GRID_W = 64
GROUP_WIDTH = D_MODEL // 4
D_FF = 4 * D_MODEL
EPS = 1e-6
CHUNK = 128
META_PAD = CHUNK - N_META

SSD_HEAD_DIM = 64
SSD_HEADS = 16
SSD_GROUPS = 2
SSD_HPG = 8
SSD_STATE = 128
SSD_CONV_W = 5
D_INNER = GROUP_WIDTH
SSD_CONV_DIM = D_INNER + 2 * SSD_GROUPS * SSD_STATE
SSD_GW = SSD_HPG * SSD_HEAD_DIM

SWA_HEAD_DIM = 128
SWA_HEADS = 8
SWA_KV_HEADS = 2
SWA_REP = SWA_HEADS // SWA_KV_HEADS
SWA_WINDOW = 128
SWA_BLOCK = 128

NA_HEAD_DIM = 64
NA_HEADS = 16
NA_KH = 8
NA_KW = 16
NA_QROWS = 4
NA_QB = NA_QROWS * GRID_W
NA_KEYS = 3 * NA_QB + CHUNK

RET_HEADS = 8
RET_K_DIM = 64
RET_V_DIM = 128

LANE = 128
COL_XBC = 0
COL_Z = 1536
COL_SWA_Q = 2560
COL_SWA_K = 3584
COL_SWA_V = 3840
COL_NA_Q = 4096
COL_NA_K = 5120
COL_NA_V = 6144
COL_RET_Q = 7168
COL_RET_K = 7680
COL_RET_V = 8192
COL_RET_G = 9216
P_COLS = 10240

NEG = -1e30
VMEM_LIMIT_BYTES = 56 * 1024 * 1024


def _params(sem):
    return pltpu.CompilerParams(dimension_semantics=sem, vmem_limit_bytes=VMEM_LIMIT_BYTES)


def _nt(a, b):
    return lax.dot_general(a, b, (((1,), (1,)), ((), ())), preferred_element_type=F32)


def _tn(a, b):
    return lax.dot_general(a, b, (((0,), (0,)), ((), ())), preferred_element_type=F32)


def _dot(a, b):
    return jnp.dot(a, b, preferred_element_type=F32)


def _dot_exact(a, b):
    return jnp.dot(a, b, preferred_element_type=F32, precision=HIGHEST)


def _silu(x):
    return x * jax.nn.sigmoid(x)


def _softplus(x):
    return jnp.maximum(x, 0.0) + jnp.log1p(jnp.exp(-jnp.abs(x)))


def _iota(shape, dim):
    return lax.broadcasted_iota(jnp.int32, shape, dim)


def _rmsnorm_kernel(x_ref, w_ref, o_ref):
    x = x_ref[...]
    ms = jnp.mean(x * x, axis=-1, keepdims=True)
    o_ref[...] = (x * lax.rsqrt(ms + EPS) * w_ref[...]).astype(o_ref.dtype)


def rmsnorm(x, w, out_dtype, tm):
    m, d = x.shape
    return pl.pallas_call(
        _rmsnorm_kernel,
        grid=(m // tm,),
        in_specs=[pl.BlockSpec((tm, d), lambda i: (i, 0)),
                  pl.BlockSpec((1, d), lambda i: (0, 0))],
        out_specs=pl.BlockSpec((tm, d), lambda i: (i, 0)),
        out_shape=jax.ShapeDtypeStruct((m, d), out_dtype),
        compiler_params=_params(("parallel",)),
        name="rmsnorm",
    )(x, w.reshape(1, d).astype(F32))


def _mm_kernel(*refs, epilogue, nk):
    if epilogue == "residual":
        x_ref, w_ref, r_ref, o_ref = refs[:4]
        scratch = refs[4:]
    else:
        x_ref, w_ref, o_ref = refs[:3]
        r_ref = None
        scratch = refs[3:]

    def finish(acc):
        if epilogue == "relu2":
            acc = jnp.square(jnp.maximum(acc, 0.0))
        elif epilogue == "residual":
            acc = acc + r_ref[...]
        o_ref[...] = acc.astype(o_ref.dtype)

    part = _dot(x_ref[...], w_ref[...])
    if nk == 1:
        finish(part)
    else:
        acc_ref = scratch[0]
        k = pl.program_id(2)

        @pl.when(k == 0)
        def _():
            acc_ref[...] = part

        @pl.when(k > 0)
        def _():
            acc_ref[...] += part

        @pl.when(k == nk - 1)
        def _():
            finish(acc_ref[...])


def matmul(x, w, *, tm, tn, tk, out_dtype, epilogue="none", residual=None):
    m, kdim = x.shape
    n = w.shape[1]
    nk = kdim // tk
    in_specs = [pl.BlockSpec((tm, tk), lambda i, j, k: (i, k)),
                pl.BlockSpec((tk, tn), lambda i, j, k: (k, j))]
    args = [x, w]
    aliases = {}
    if epilogue == "residual":
        in_specs.append(pl.BlockSpec((tm, tn), lambda i, j, k: (i, j)))
        args.append(residual)
        aliases = {2: 0}
    scratch = [pltpu.VMEM((tm, tn), F32)] if nk > 1 else []
    return pl.pallas_call(
        functools.partial(_mm_kernel, epilogue=epilogue, nk=nk),
        grid=(m // tm, n // tn, nk),
        in_specs=in_specs,
        out_specs=pl.BlockSpec((tm, tn), lambda i, j, k: (i, j)),
        out_shape=jax.ShapeDtypeStruct((m, n), out_dtype),
        scratch_shapes=scratch,
        input_output_aliases=aliases,
        compiler_params=_params(("parallel", "parallel", "arbitrary")),
        name="matmul_" + epilogue,
    )(*args)


class Group:
    def __init__(self, base, mbase, bsz, t):
        self.base = base
        self.mbase = mbase
        self.bsz = bsz
        self.t = t
        self.nc = t // CHUNK
        self.nc1 = self.nc + 1
        self.lp = self.nc1 * CHUNK

    def rblk(self, rows):
        per = self.t // rows
        first = self.base // rows
        return lambda b: first + b * per

    def mblk(self):
        first = self.mbase // N_META
        return lambda b: first + b


def _meta_front(m_val, width, dtype):
    return jnp.concatenate([jnp.zeros((META_PAD, width), dtype), m_val], axis=0)


def _ssd_pre_kernel(cur_ref, prev_ref, next_ref, meta_ref, dtr_ref, dtm_ref, cw_ref, cb_ref,
                    bias4_ref, alog4_ref,
                    act_ref, cs_ref, dtv_ref, cst_ref, dtt_ref, tot_ref, ext_ref, *, nc):
    c = pl.program_id(1)
    is_meta = c == 0
    w = SSD_CONV_DIM
    meta = meta_ref[...].astype(F32)
    cur = jnp.where(is_meta, _meta_front(meta, w, F32), cur_ref[...].astype(F32))
    prev = jnp.where(is_meta, 0.0, jnp.where(c == 1, meta, prev_ref[...].astype(F32)))
    nxt = jnp.where(c == nc, 0.0, next_ref[...].astype(F32))
    ext_ref[0:N_META, :] = prev
    ext_ref[N_META:N_META + CHUNK, :] = cur
    ext_ref[N_META + CHUNK:, :] = nxt
    half = (SSD_CONV_W - 1) // 2
    acc = jnp.broadcast_to(cb_ref[...], (CHUNK, w))
    for j in range(SSD_CONV_W):
        acc = acc + cw_ref[j:j + 1, :] * ext_ref[pl.ds(N_META - half + j, CHUNK), :]
    row = _iota((CHUNK, 1), 0)
    valid = jnp.logical_or(c > 0, row >= META_PAD)
    act_ref[...] = jnp.where(valid, _silu(acc), 0.0).astype(act_ref.dtype)

    dtx = jnp.where(is_meta, _meta_front(dtm_ref[...], LANE, F32), dtr_ref[...])
    lane = _iota((CHUNK, LANE), 1)
    keep = jnp.logical_and(valid, lane < SSD_HPG)
    ti = _iota((CHUNK, CHUNK), 0)
    si = _iota((CHUNK, CHUNK), 1)
    tri = (si <= ti).astype(F32)
    for k in range(2 * SSD_GROUPS):
        x = dtx if k == 0 else pltpu.roll(dtx, LANE - SSD_HPG * k, axis=1)
        dt = jnp.where(keep, _softplus(x + bias4_ref[k]), 0.0)
        la = dt * (-jnp.exp(alog4_ref[k]))
        incl = _dot_exact(tri, la)
        cs = incl if k < SSD_GROUPS else incl - la
        cs_ref[k] = cs
        dtv_ref[k] = dt
        cst_ref[k] = jnp.transpose(cs)[0:SSD_HPG, :]
        dtt_ref[k] = jnp.transpose(dt)[0:SSD_HPG, :]
        tot_ref[k] = jnp.broadcast_to(incl[CHUNK - 1:CHUNK, :], (8, LANE))


def ssd_pre(pr, pm, dtr, dtm, conv_w, conv_b, bias4, alog4, grp):
    nc, nc1, bsz = grp.nc, grp.nc1, grp.bsz
    w = SSD_CONV_DIM
    r128, r16, mb = grp.rblk(CHUNK), grp.rblk(N_META), grp.mblk()
    per16 = CHUNK // N_META
    ng = 2 * SSD_GROUPS
    in_specs = [
        pl.BlockSpec((CHUNK, w), lambda b, c: (r128(b) + jnp.maximum(c - 1, 0), 0)),
        pl.BlockSpec((N_META, w), lambda b, c: (r16(b) + jnp.maximum((c - 1) * per16 - 1, 0), 0)),
        pl.BlockSpec((N_META, w), lambda b, c: (r16(b) + jnp.minimum(c * per16, nc * per16 - 1), 0)),
        pl.BlockSpec((N_META, w), lambda b, c: (mb(b), 0)),
        pl.BlockSpec((CHUNK, LANE), lambda b, c: (r128(b) + jnp.maximum(c - 1, 0), 0)),
        pl.BlockSpec((N_META, LANE), lambda b, c: (mb(b), 0)),
        pl.BlockSpec((8, w), lambda b, c: (0, 0)),
        pl.BlockSpec((1, w), lambda b, c: (0, 0)),
        pl.BlockSpec((ng, 1, LANE), lambda b, c: (0, 0, 0)),
        pl.BlockSpec((ng, 1, LANE), lambda b, c: (0, 0, 0)),
    ]
    out_specs = [
        pl.BlockSpec((None, CHUNK, w), lambda b, c: (b, c, 0)),
        pl.BlockSpec((None, ng, CHUNK, LANE), lambda b, c: (b, 0, c, 0)),
        pl.BlockSpec((None, ng, CHUNK, LANE), lambda b, c: (b, 0, c, 0)),
        pl.BlockSpec((None, ng, None, SSD_HPG, LANE), lambda b, c: (b, 0, c, 0, 0)),
        pl.BlockSpec((None, ng, None, SSD_HPG, LANE), lambda b, c: (b, 0, c, 0, 0)),
        pl.BlockSpec((None, ng, None, 8, LANE), lambda b, c: (b, 0, c, 0, 0)),
    ]
    out_shape = [
        jax.ShapeDtypeStruct((bsz, grp.lp, w), BF16),
        jax.ShapeDtypeStruct((bsz, ng, grp.lp, LANE), F32),
        jax.ShapeDtypeStruct((bsz, ng, grp.lp, LANE), F32),
        jax.ShapeDtypeStruct((bsz, ng, nc1, SSD_HPG, LANE), F32),
        jax.ShapeDtypeStruct((bsz, ng, nc1, SSD_HPG, LANE), F32),
        jax.ShapeDtypeStruct((bsz, ng, nc1, 8, LANE), F32),
    ]
    return pl.pallas_call(
        functools.partial(_ssd_pre_kernel, nc=nc),
        grid=(bsz, nc1),
        in_specs=in_specs, out_specs=out_specs, out_shape=out_shape,
        scratch_shapes=[pltpu.VMEM((CHUNK + 2 * N_META, w), F32)],
        compiler_params=_params(("parallel", "arbitrary")),
        name="ssd_pre",
    )(pr, pr, pr, pm, dtr, dtm, conv_w, conv_b, bias4, alog4)


def _ssd_scan_kernel(*refs, rev, final):
    if final:
        (xs_ref, b_ref, c_ref, cs_ref, dtv_ref, cst_ref, dtt_ref, tot_ref,
         yb_ref, zr_ref, zm_ref, dsk_ref, nw_ref, o_ref, s_ref) = refs
    else:
        (xs_ref, b_ref, c_ref, cs_ref, dtv_ref, cst_ref, dtt_ref, tot_ref, o_ref, s_ref) = refs
    step = pl.program_id(2)

    @pl.when(step == 0)
    def _():
        s_ref[...] = jnp.zeros_like(s_ref)

    xs = xs_ref[...]
    bm = b_ref[...]
    cm = c_ref[...]
    a = cs_ref[...]
    dt = dtv_ref[...]
    a_t = cst_ref[...]
    dt_t = dtt_ref[...]
    tot = tot_ref[0:1, :]

    g = _nt(cm, bm)
    ti = _iota((CHUNK, CHUNK), 0)
    si = _iota((CHUNK, CHUNK), 1)
    mask = (si > ti) if rev else (si <= ti)
    lane = _iota((CHUNK, LANE), 1)
    zero = jnp.zeros((), xs.dtype)
    pairs = []
    for p in range(SSD_HPG // 2):
        xp = xs[:, p * LANE:(p + 1) * LANE]
        acc = None
        for q in range(2):
            h = 2 * p + q
            acol = a[:, h:h + 1]
            arow = a_t[h:h + 1, :]
            e = (arow - acol) if rev else (acol - arow)
            wm = g * jnp.exp(jnp.where(mask, e, NEG)) * dt_t[h:h + 1, :]
            xm = jnp.where((lane >= SSD_HEAD_DIM) if q else (lane < SSD_HEAD_DIM), xp, zero)
            t = _dot(wm.astype(BF16), xm)
            acc = t if acc is None else acc + t
        pairs.append(acc)
    y = jnp.concatenate(pairs, axis=1)

    sel = (_iota((LANE, SSD_GW), 0) == _iota((LANE, SSD_GW), 1) // SSD_HEAD_DIM).astype(F32)
    if rev:
        e1 = jnp.exp(tot - a)
        wst = dt * jnp.exp(a)
    else:
        e1 = jnp.exp(a)
        wst = dt * jnp.exp(tot - a)
    e1x = _dot_exact(e1, sel)
    wx = _dot_exact(wst, sel)
    decx = _dot_exact(jnp.broadcast_to(jnp.exp(tot), (8, LANE)), sel)[0:1, :]
    state = s_ref[...]
    y = y + e1x * _dot(cm, state.astype(BF16))
    xw = (xs.astype(F32) * wx).astype(BF16)
    s_ref[...] = decx * state + _tn(bm, xw)

    if final:
        z = jnp.where(step == 0, _meta_front(zm_ref[...], SSD_GW, BF16), zr_ref[...]).astype(F32)
        yt = y + yb_ref[...] + xs.astype(F32) * dsk_ref[...]
        yt = yt * _silu(z)
        ms = jnp.mean(yt * yt, axis=-1, keepdims=True)
        o_ref[...] = (yt * lax.rsqrt(ms + EPS) * nw_ref[...]).astype(o_ref.dtype)
    else:
        o_ref[...] = y


def ssd_scan(act, cs4, dt4, cst4, dtt4, tot4, grp, *, rev, ybwd=None, pr=None, pm=None,
             d_skip=None, norm_w=None):
    final = not rev
    nc1, bsz = grp.nc1, grp.bsz
    d = 1 if rev else 0
    cc = (lambda c: nc1 - 1 - c) if rev else (lambda c: c)
    xs_blk = COL_XBC // SSD_GW
    b_blk = (COL_XBC + D_INNER) // LANE
    c_blk = b_blk + SSD_GROUPS
    in_specs = [
        pl.BlockSpec((None, CHUNK, SSD_GW), lambda b, g, c: (b, cc(c), xs_blk + g)),
        pl.BlockSpec((None, CHUNK, LANE), lambda b, g, c: (b, cc(c), b_blk + g)),
        pl.BlockSpec((None, CHUNK, LANE), lambda b, g, c: (b, cc(c), c_blk + g)),
        pl.BlockSpec((None, None, CHUNK, LANE), lambda b, g, c: (b, d * SSD_GROUPS + g, cc(c), 0)),
        pl.BlockSpec((None, None, CHUNK, LANE), lambda b, g, c: (b, d * SSD_GROUPS + g, cc(c), 0)),
        pl.BlockSpec((None, None, None, SSD_HPG, LANE), lambda b, g, c: (b, d * SSD_GROUPS + g, cc(c), 0, 0)),
        pl.BlockSpec((None, None, None, SSD_HPG, LANE), lambda b, g, c: (b, d * SSD_GROUPS + g, cc(c), 0, 0)),
        pl.BlockSpec((None, None, None, 8, LANE), lambda b, g, c: (b, d * SSD_GROUPS + g, cc(c), 0, 0)),
    ]
    args = [act, act, act, cs4, dt4, cst4, dtt4, tot4]
    if final:
        r128, mb = grp.rblk(CHUNK), grp.mblk()
        z_blk = COL_Z // SSD_GW
        in_specs += [
            pl.BlockSpec((None, CHUNK, SSD_GW), lambda b, g, c: (b, c, g)),
            pl.BlockSpec((CHUNK, SSD_GW), lambda b, g, c: (r128(b) + jnp.maximum(c - 1, 0), z_blk + g)),
            pl.BlockSpec((N_META, SSD_GW), lambda b, g, c: (mb(b), z_blk + g)),
            pl.BlockSpec((1, SSD_GW), lambda b, g, c: (0, g)),
            pl.BlockSpec((1, SSD_GW), lambda b, g, c: (0, g)),
        ]
        args += [ybwd, pr, pm, d_skip, norm_w]
        out_dtype = BF16
    else:
        out_dtype = F32
    return pl.pallas_call(
        functools.partial(_ssd_scan_kernel, rev=rev, final=final),
        grid=(bsz, SSD_GROUPS, nc1),
        in_specs=in_specs,
        out_specs=pl.BlockSpec((None, CHUNK, SSD_GW), lambda b, g, c: (b, cc(c), g)),
        out_shape=jax.ShapeDtypeStruct((bsz, grp.lp, D_INNER), out_dtype),
        scratch_shapes=[pltpu.VMEM((SSD_STATE, SSD_GW), F32)],
        compiler_params=_params(("parallel", "parallel", "arbitrary")),
        name="ssd_scan_" + ("bwd" if rev else "fwd"),
    )(*args)


def _ret_scan_kernel(*refs, rev, final, nc1):
    if final:
        (dl_ref, qr_ref, qm_ref, kr_ref, km_ref, vr_ref, vm_ref, yb_ref, gr_ref, gm_ref,
         o_ref, s_ref) = refs
    else:
        dl_ref, qr_ref, qm_ref, kr_ref, km_ref, vr_ref, vm_ref, o_ref, s_ref = refs
    h = pl.program_id(1)
    step = pl.program_id(2)
    chunk = (nc1 - 1 - step) if rev else step
    is_meta = chunk == 0

    @pl.when(step == 0)
    def _():
        s_ref[...] = jnp.zeros_like(s_ref)

    def pick(r_ref, m_ref):
        return jnp.where(is_meta, _meta_front(m_ref[...], LANE, BF16), r_ref[...])

    q = pick(qr_ref, qm_ref)
    k = pick(kr_ref, km_ref)
    v = pick(vr_ref, vm_ref)
    lane = _iota((CHUNK, LANE), 1)
    odd = (h % 2) == 1
    mine = (lane >= RET_K_DIM) == odd
    qh = jnp.where(mine, q, jnp.zeros((), q.dtype)) * jnp.asarray(RET_K_DIM ** -0.5, q.dtype)

    x = jnp.full((CHUNK, CHUNK), dl_ref[1 if rev else 0, h], F32)
    lg = jnp.minimum(x, 0.0) - jnp.log1p(jnp.exp(-jnp.abs(x)))
    ti = _iota((CHUNK, CHUNK), 0)
    si = _iota((CHUNK, CHUNK), 1)
    tf = ti.astype(F32)
    if rev:
        dmat = jnp.exp(jnp.where(si > ti, (si - ti).astype(F32) * lg, NEG))
        e1 = jnp.exp((CHUNK - tf) * lg)
        wst = jnp.exp(tf * lg)
    else:
        dmat = jnp.exp(jnp.where(si <= ti, (ti - si).astype(F32) * lg, NEG))
        e1 = jnp.exp((tf + 1.0) * lg)
        wst = jnp.exp((CHUNK - 1.0 - tf) * lg)
    sc = _nt(qh, k)
    y = _dot((sc * dmat).astype(BF16), v)
    state = s_ref[...]
    y = y + e1 * _dot(qh, state.astype(BF16))
    vw = (v.astype(F32) * wst).astype(BF16)
    s_ref[...] = jnp.exp(CHUNK * lg) * state + _tn(k, vw)

    if final:
        gate = pick(gr_ref, gm_ref).astype(F32)
        yt = y + yb_ref[...]
        mu = jnp.mean(yt, axis=-1, keepdims=True)
        dev = yt - mu
        var = jnp.mean(dev * dev, axis=-1, keepdims=True)
        o_ref[...] = (dev * lax.rsqrt(var + 1e-5) * _silu(gate)).astype(o_ref.dtype)
    else:
        o_ref[...] = y


def ret_scan(pr, pm, decay, grp, *, rev, ybwd=None):
    final = not rev
    nc1, bsz = grp.nc1, grp.bsz
    r128, mb = grp.rblk(CHUNK), grp.mblk()
    cc = (lambda c: nc1 - 1 - c) if rev else (lambda c: c)

    def real(col0, per_pair):
        blk = col0 // LANE
        if per_pair:
            return pl.BlockSpec((CHUNK, LANE), lambda b, h, c: (r128(b) + jnp.maximum(cc(c) - 1, 0), blk + h // 2))
        return pl.BlockSpec((CHUNK, LANE), lambda b, h, c: (r128(b) + jnp.maximum(cc(c) - 1, 0), blk + h))

    def meta(col0, per_pair):
        blk = col0 // LANE
        if per_pair:
            return pl.BlockSpec((N_META, LANE), lambda b, h, c: (mb(b), blk + h // 2))
        return pl.BlockSpec((N_META, LANE), lambda b, h, c: (mb(b), blk + h))

    in_specs = [pl.BlockSpec(memory_space=pltpu.SMEM),
                real(COL_RET_Q, True), meta(COL_RET_Q, True),
                real(COL_RET_K, True), meta(COL_RET_K, True),
                real(COL_RET_V, False), meta(COL_RET_V, False)]
    args = [decay, pr, pm, pr, pm, pr, pm]
    if final:
        in_specs += [pl.BlockSpec((None, CHUNK, LANE), lambda b, h, c: (b, c, h)),
                     real(COL_RET_G, False), meta(COL_RET_G, False)]
        args += [ybwd, pr, pm]
        out_dtype = BF16
    else:
        out_dtype = F32
    return pl.pallas_call(
        functools.partial(_ret_scan_kernel, rev=rev, final=final, nc1=nc1),
        grid=(bsz, RET_HEADS, nc1),
        in_specs=in_specs,
        out_specs=pl.BlockSpec((None, CHUNK, LANE), lambda b, h, c: (b, cc(c), h)),
        out_shape=jax.ShapeDtypeStruct((bsz, grp.lp, GROUP_WIDTH), out_dtype),
        scratch_shapes=[pltpu.VMEM((LANE, RET_V_DIM), F32)],
        compiler_params=_params(("parallel", "parallel", "arbitrary")),
        name="ret_scan_" + ("bwd" if rev else "fwd"),
    )(*args)


def _alibi_slope_rows(rows_per_head, nrows, g):
    hd = _iota((nrows, 1), 0) // rows_per_head
    s = jnp.full((nrows, 1), 2.0 ** -SWA_REP, F32)
    for r in range(SWA_REP - 1):
        s = jnp.where(hd == r, 2.0 ** -(r + 1), s)
    return s * jnp.where(g == 1, 2.0 ** -SWA_REP, 1.0)


def _sink_rows(sink_ref, rows_per_head, nrows, g):
    hd = _iota((nrows, 1), 0) // rows_per_head
    s = jnp.full((nrows, 1), sink_ref[g * SWA_REP + SWA_REP - 1], F32)
    for r in range(SWA_REP - 1):
        s = jnp.where(hd == r, sink_ref[g * SWA_REP + r], s)
    return s


def _softmax_with_sink(s, sink, vall):
    m = jnp.maximum(jnp.max(s, axis=-1, keepdims=True), sink)
    p = jnp.exp(s - m)
    denom = jnp.sum(p, axis=-1, keepdims=True) + jnp.exp(sink - m)
    return _dot(p.astype(BF16), vall) / denom


def _swa_kernel(sink_ref, q_ref, kp_ref, ko_ref, kn_ref, vp_ref, vo_ref, vn_ref, km_ref, vm_ref,
                o_ref, bias_ref, *, nb):
    g = pl.program_id(1)
    n = pl.program_id(2)
    blk = SWA_BLOCK
    nq = SWA_REP * blk
    nkeys = 4 * blk

    @pl.when(n == 0)
    def _():
        t = _iota((nq, nkeys), 0) % blk
        col = _iota((nq, nkeys), 1)
        rel = col - blk - t
        dist = jnp.abs(rel)
        band = jnp.logical_and(col < 3 * blk, dist <= SWA_WINDOW)
        is_meta = jnp.logical_and(col >= 3 * blk, col < 3 * blk + N_META)
        slope = _alibi_slope_rows(blk, nq, g)
        bias_ref[...] = jnp.where(band, -slope * dist.astype(F32), jnp.where(is_meta, 0.0, NEG))

    q = q_ref[...]
    qs = jnp.concatenate([q[:, r * blk:(r + 1) * blk] for r in range(SWA_REP)], axis=0)
    pad = jnp.zeros((META_PAD, SWA_HEAD_DIM), BF16)
    kall = jnp.concatenate([kp_ref[...], ko_ref[...], kn_ref[...], km_ref[...], pad], axis=0)
    vall = jnp.concatenate([vp_ref[...], vo_ref[...], vn_ref[...], vm_ref[...], pad], axis=0)
    s = _nt(qs, kall) * (SWA_HEAD_DIM ** -0.5) + bias_ref[...]
    col = _iota((nq, nkeys), 1)
    lo = jnp.where(n == 0, blk, 0)
    hi = jnp.where(n == nb - 1, 2 * blk, 3 * blk)
    outside = jnp.logical_or(col < lo, jnp.logical_and(col >= hi, col < 3 * blk))
    s = jnp.where(outside, NEG, s)
    o = _softmax_with_sink(s, _sink_rows(sink_ref, blk, nq, g), vall)
    o_ref[...] = jnp.concatenate([o[r * blk:(r + 1) * blk] for r in range(SWA_REP)],
                                 axis=1).astype(o_ref.dtype)


def swa_real(pr, pm, sink, grp):
    bsz, nb = grp.bsz, grp.t // SWA_BLOCK
    r128, mb = grp.rblk(SWA_BLOCK), grp.mblk()
    qw = SWA_REP * SWA_HEAD_DIM
    q_blk, k_blk, v_blk = COL_SWA_Q // qw, COL_SWA_K // LANE, COL_SWA_V // LANE

    def kv(col_blk, off):
        return pl.BlockSpec((SWA_BLOCK, LANE),
                            lambda b, g, n: (r128(b) + jnp.clip(n + off, 0, nb - 1), col_blk + g))

    in_specs = [pl.BlockSpec(memory_space=pltpu.SMEM),
                pl.BlockSpec((SWA_BLOCK, qw), lambda b, g, n: (r128(b) + n, q_blk + g)),
                kv(k_blk, -1), kv(k_blk, 0), kv(k_blk, 1),
                kv(v_blk, -1), kv(v_blk, 0), kv(v_blk, 1),
                pl.BlockSpec((N_META, LANE), lambda b, g, n: (mb(b), k_blk + g)),
                pl.BlockSpec((N_META, LANE), lambda b, g, n: (mb(b), v_blk + g))]
    return pl.pallas_call(
        functools.partial(_swa_kernel, nb=nb),
        grid=(bsz, SWA_KV_HEADS, nb),
        in_specs=in_specs,
        out_specs=pl.BlockSpec((SWA_BLOCK, qw), lambda b, g, n: (b * nb + n, g)),
        out_shape=jax.ShapeDtypeStruct((bsz * grp.t, GROUP_WIDTH), BF16),
        scratch_shapes=[pltpu.VMEM((SWA_REP * SWA_BLOCK, 4 * SWA_BLOCK), F32)],
        compiler_params=_params(("parallel", "parallel", "arbitrary")),
        name="swa_real",
    )(sink, pr, pr, pr, pr, pr, pr, pr, pm, pm)


def _swa_meta_kernel(sink_ref, q_ref, k0_ref, v0_ref, km_ref, vm_ref, o_ref):
    g = pl.program_id(1)
    nq = SWA_REP * N_META
    nkeys = 2 * SWA_BLOCK
    q = q_ref[...]
    qs = jnp.concatenate([q[:, r * LANE:(r + 1) * LANE] for r in range(SWA_REP)], axis=0)
    pad = jnp.zeros((META_PAD, SWA_HEAD_DIM), BF16)
    kall = jnp.concatenate([k0_ref[...], km_ref[...], pad], axis=0)
    vall = jnp.concatenate([v0_ref[...], vm_ref[...], pad], axis=0)
    i = _iota((nq, nkeys), 0) % N_META
    col = _iota((nq, nkeys), 1)
    mdist = N_META + col - i
    near = jnp.logical_and(col < SWA_BLOCK, mdist <= SWA_WINDOW)
    is_meta = jnp.logical_and(col >= SWA_BLOCK, col < SWA_BLOCK + N_META)
    slope = _alibi_slope_rows(N_META, nq, g)
    bias = jnp.where(near, -slope * mdist.astype(F32), jnp.where(is_meta, 0.0, NEG))
    s = _nt(qs, kall) * (SWA_HEAD_DIM ** -0.5) + bias
    o = _softmax_with_sink(s, _sink_rows(sink_ref, N_META, nq, g), vall)
    o_ref[...] = jnp.concatenate([o[r * N_META:(r + 1) * N_META] for r in range(SWA_REP)],
                                 axis=1).astype(o_ref.dtype)


def swa_meta(pr, pm, sink, grp):
    bsz = grp.bsz
    r128, mb = grp.rblk(SWA_BLOCK), grp.mblk()
    qw = SWA_REP * SWA_HEAD_DIM
    q_blk, k_blk, v_blk = COL_SWA_Q // qw, COL_SWA_K // LANE, COL_SWA_V // LANE
    in_specs = [pl.BlockSpec(memory_space=pltpu.SMEM),
                pl.BlockSpec((N_META, qw), lambda b, g: (mb(b), q_blk + g)),
                pl.BlockSpec((SWA_BLOCK, LANE), lambda b, g: (r128(b), k_blk + g)),
                pl.BlockSpec((SWA_BLOCK, LANE), lambda b, g: (r128(b), v_blk + g)),
                pl.BlockSpec((N_META, LANE), lambda b, g: (mb(b), k_blk + g)),
                pl.BlockSpec((N_META, LANE), lambda b, g: (mb(b), v_blk + g))]
    return pl.pallas_call(
        _swa_meta_kernel,
        grid=(bsz, SWA_KV_HEADS),
        in_specs=in_specs,
        out_specs=pl.BlockSpec((N_META, qw), lambda b, g: (b, g)),
        out_shape=jax.ShapeDtypeStruct((bsz * N_META, GROUP_WIDTH), BF16),
        compiler_params=_params(("parallel", "parallel")),
        name="swa_meta",
    )(sink, pm, pr, pr, pm, pm)


def _na_tables(rpb):
    qi = np.arange(NA_QB) // GRID_W
    qc = np.arange(NA_QB) % GRID_W
    kj = np.arange(3 * NA_QB) // GRID_W
    kc = np.arange(3 * NA_QB) % GRID_W
    col_start = np.clip(qc - NA_KW // 2, 0, GRID_W - NA_KW)
    col_ok = (kc[None, :] >= col_start[:, None]) & (kc[None, :] < col_start[:, None] + NA_KW)
    dr = kj[None, :] - NA_QROWS - qi[:, None] + NA_KH - 1
    dc = kc[None, :] - qc[:, None] + NA_KW - 1
    ok = col_ok & (dr >= 0) & (dr <= 2 * NA_KH - 2)
    dr_c = np.clip(dr, 0, 2 * NA_KH - 2)
    dc_c = np.clip(dc, 0, 2 * NA_KW - 2)
    band = jnp.where(ok[None], rpb.astype(F32)[:, dr_c, dc_c], NEG)
    meta_cols = np.where(np.arange(CHUNK) < N_META, 0.0, NEG).astype(np.float32)
    meta_part = jnp.broadcast_to(jnp.asarray(meta_cols), (NA_HEADS, NA_QB, CHUNK))
    bias = jnp.concatenate([band, meta_part], axis=-1)

    def rows(lo, hi):
        okr = (kj[None, :] >= lo[:, None]) & (kj[None, :] < hi[:, None])
        m = np.where(okr, 0.0, NEG).astype(np.float32)
        return np.concatenate([m, np.zeros((NA_QB, CHUNK), np.float32)], axis=1)

    first = rows(np.full(NA_QB, NA_QROWS), np.full(NA_QB, NA_QROWS + NA_KH))
    inner = rows(qi, qi + NA_KH)
    last = rows(np.zeros(NA_QB, np.int64), np.full(NA_QB, NA_KH))
    return bias, jnp.asarray(np.stack([first, inner, last]))


def _na_kernel(q_ref, kp_ref, ko_ref, kn_ref, vp_ref, vo_ref, vn_ref, km_ref, vm_ref,
               bias_ref, rmask_ref, o_ref):
    q = q_ref[...]
    pad = jnp.zeros((META_PAD, LANE), BF16)
    kall = jnp.concatenate([kp_ref[...], ko_ref[...], kn_ref[...], km_ref[...], pad], axis=0)
    vall = jnp.concatenate([vp_ref[...], vo_ref[...], vn_ref[...], vm_ref[...], pad], axis=0)
    zero = jnp.zeros((), BF16)
    qlane = _iota((NA_QB, LANE), 1)
    vlane = _iota((NA_KEYS, LANE), 1)
    rmask = rmask_ref[...]
    out = jnp.zeros((NA_QB, LANE), F32)
    for hh in range(2):
        qsel = (qlane >= NA_HEAD_DIM) if hh else (qlane < NA_HEAD_DIM)
        vsel = (vlane >= NA_HEAD_DIM) if hh else (vlane < NA_HEAD_DIM)
        qh = jnp.where(qsel, q, zero)
        s = _nt(qh, kall) * (NA_HEAD_DIM ** -0.5) + bias_ref[hh] + rmask
        m = jnp.max(s, axis=-1, keepdims=True)
        p = jnp.exp(s - m)
        denom = jnp.sum(p, axis=-1, keepdims=True)
        out = out + _dot(p.astype(BF16), jnp.where(vsel, vall, zero)) / denom
    o_ref[...] = out.astype(o_ref.dtype)


def na_real(pr, pm, bias, rmask, grp):
    bsz, nqb = grp.bsz, grp.t // NA_QB
    rq, mb = grp.rblk(NA_QB), grp.mblk()
    q_blk, k_blk, v_blk = COL_NA_Q // LANE, COL_NA_K // LANE, COL_NA_V // LANE
    npair = NA_HEADS // 2

    def kv(col_blk, off):
        return pl.BlockSpec((NA_QB, LANE),
                            lambda p, b, n: (rq(b) + jnp.clip(n + off, 0, nqb - 1), col_blk + p))

    in_specs = [pl.BlockSpec((NA_QB, LANE), lambda p, b, n: (rq(b) + n, q_blk + p)),
                kv(k_blk, -1), kv(k_blk, 0), kv(k_blk, 1),
                kv(v_blk, -1), kv(v_blk, 0), kv(v_blk, 1),
                pl.BlockSpec((N_META, LANE), lambda p, b, n: (mb(b), k_blk + p)),
                pl.BlockSpec((N_META, LANE), lambda p, b, n: (mb(b), v_blk + p)),
                pl.BlockSpec((2, NA_QB, NA_KEYS), lambda p, b, n: (p, 0, 0)),
                pl.BlockSpec((None, NA_QB, NA_KEYS),
                             lambda p, b, n: (jnp.where(n == 0, 0, jnp.where(n == nqb - 1, 2, 1)), 0, 0))]
    return pl.pallas_call(
        _na_kernel,
        grid=(npair, bsz, nqb),
        in_specs=in_specs,
        out_specs=pl.BlockSpec((NA_QB, LANE), lambda p, b, n: (b * nqb + n, p)),
        out_shape=jax.ShapeDtypeStruct((bsz * grp.t, GROUP_WIDTH), BF16),
        compiler_params=_params(("parallel", "parallel", "arbitrary")),
        name="na_real",
    )(pr, pr, pr, pr, pr, pr, pr, pm, pm, bias, rmask)


def _na_meta_kernel(q_ref, k_ref, v_ref, km_ref, vm_ref, o_ref):
    nwin = NA_KH * GRID_W
    nkeys = nwin + CHUNK
    q = q_ref[...]
    pad = jnp.zeros((META_PAD, LANE), BF16)
    kall = jnp.concatenate([k_ref[...], km_ref[...], pad], axis=0)
    vall = jnp.concatenate([v_ref[...], vm_ref[...], pad], axis=0)
    col = _iota((N_META, nkeys), 1)
    in_win = jnp.logical_and(col < nwin, col % GRID_W < NA_KW)
    is_meta = jnp.logical_and(col >= nwin, col < nwin + N_META)
    bias = jnp.where(jnp.logical_or(in_win, is_meta), 0.0, NEG)
    zero = jnp.zeros((), BF16)
    qlane = _iota((N_META, LANE), 1)
    vlane = _iota((nkeys, LANE), 1)
    out = jnp.zeros((N_META, LANE), F32)
    for hh in range(2):
        qsel = (qlane >= NA_HEAD_DIM) if hh else (qlane < NA_HEAD_DIM)
        vsel = (vlane >= NA_HEAD_DIM) if hh else (vlane < NA_HEAD_DIM)
        s = _nt(jnp.where(qsel, q, zero), kall) * (NA_HEAD_DIM ** -0.5) + bias
        m = jnp.max(s, axis=-1, keepdims=True)
        p = jnp.exp(s - m)
        denom = jnp.sum(p, axis=-1, keepdims=True)
        out = out + _dot(p.astype(BF16), jnp.where(vsel, vall, zero)) / denom
    o_ref[...] = out.astype(o_ref.dtype)


def na_meta(pr, pm, grp):
    bsz = grp.bsz
    nwin = NA_KH * GRID_W
    rw, mb = grp.rblk(nwin), grp.mblk()
    q_blk, k_blk, v_blk = COL_NA_Q // LANE, COL_NA_K // LANE, COL_NA_V // LANE
    in_specs = [pl.BlockSpec((N_META, LANE), lambda b, p: (mb(b), q_blk + p)),
                pl.BlockSpec((nwin, LANE), lambda b, p: (rw(b), k_blk + p)),
                pl.BlockSpec((nwin, LANE), lambda b, p: (rw(b), v_blk + p)),
                pl.BlockSpec((N_META, LANE), lambda b, p: (mb(b), k_blk + p)),
                pl.BlockSpec((N_META, LANE), lambda b, p: (mb(b), v_blk + p))]
    return pl.pallas_call(
        _na_meta_kernel,
        grid=(bsz, NA_HEADS // 2),
        in_specs=in_specs,
        out_specs=pl.BlockSpec((N_META, LANE), lambda b, p: (b, p)),
        out_shape=jax.ShapeDtypeStruct((bsz * N_META, GROUP_WIDTH), BF16),
        compiler_params=_params(("parallel", "parallel")),
        name="na_meta",
    )(pm, pr, pr, pm, pm)


def mix_group(pr, pm, dtr, dtm, lw, grp, need_meta):
    bsz, t = grp.bsz, grp.t
    act, cs4, dt4, cst4, dtt4, tot4 = ssd_pre(pr, pm, dtr, dtm, lw["conv_w"], lw["conv_b"],
                                               lw["bias4"], lw["alog4"], grp)
    yb = ssd_scan(act, cs4, dt4, cst4, dtt4, tot4, grp, rev=True)
    y_ssd = ssd_scan(act, cs4, dt4, cst4, dtt4, tot4, grp, rev=False, ybwd=yb, pr=pr, pm=pm,
                     d_skip=lw["d_skip"], norm_w=lw["ssd_norm_w"])
    rb = ret_scan(pr, pm, lw["ret_decay"], grp, rev=True)
    y_ret = ret_scan(pr, pm, lw["ret_decay"], grp, rev=False, ybwd=rb)
    o_swa = swa_real(pr, pm, lw["sink"], grp)
    o_na = na_real(pr, pm, lw["na_bias"], lw["na_rmask"], grp)
    real = [y_ssd[:, CHUNK:].reshape(bsz * t, GROUP_WIDTH), o_swa, o_na,
            y_ret[:, CHUNK:].reshape(bsz * t, GROUP_WIDTH)]
    meta = None
    if need_meta:
        meta = [y_ssd[:, META_PAD:CHUNK].reshape(bsz * N_META, GROUP_WIDTH),
                swa_meta(pr, pm, lw["sink"], grp), na_meta(pr, pm, grp),
                y_ret[:, META_PAD:CHUNK].reshape(bsz * N_META, GROUP_WIDTH)]
    return real, meta


def _layer_weights(i, w_in, ssd_conv_w, ssd_conv_b, ssd_dt_bias, ssd_a_log, ssd_d, ssd_norm_w,
                   swa_sink, na_rpb, ret_decay, w_out, w_up, w_down):
    wi = w_in[i]
    z0, x0 = 0, D_INNER
    dt0 = D_INNER + SSD_CONV_DIM
    rest0 = dt0 + 2 * SSD_HEADS
    w_main = jnp.concatenate([wi[:, x0:dt0], wi[:, z0:x0], wi[:, rest0:]], axis=1).astype(BF16)
    w_dt = jnp.pad(wi[:, dt0:rest0], ((0, 0), (0, LANE - 2 * SSD_HEADS))).astype(BF16)

    def per_dir_group(v):
        v4 = v.astype(F32).reshape(2 * SSD_GROUPS, 1, SSD_HPG)
        return jnp.pad(v4, ((0, 0), (0, 0), (0, LANE - SSD_HPG)))

    na_bias, na_rmask = _na_tables(na_rpb[i])
    return dict(
        w_main=w_main, w_dt=w_dt,
        conv_w=jnp.pad(ssd_conv_w[i].astype(F32), ((0, 8 - SSD_CONV_W), (0, 0))),
        conv_b=ssd_conv_b[i].astype(F32).reshape(1, SSD_CONV_DIM),
        bias4=per_dir_group(ssd_dt_bias[i]), alog4=per_dir_group(ssd_a_log[i]),
        d_skip=jnp.repeat(ssd_d[i].astype(F32), SSD_HEAD_DIM).reshape(1, D_INNER),
        ssd_norm_w=ssd_norm_w[i].astype(F32).reshape(1, D_INNER),
        sink=swa_sink[i].astype(F32), na_bias=na_bias, na_rmask=na_rmask,
        ret_decay=ret_decay[i].astype(F32),
        w_out=w_out[i].astype(BF16), w_up=w_up[i].astype(BF16), w_down=w_down[i].astype(BF16),
    )


TM_REAL = 1024
TN = 512
TK_DOWN = 2048
TM_NORM = 256


def kernel(x_prompt, x_sample, meta_tokens, norm1_w, w_in, ssd_conv_w, ssd_conv_b, ssd_dt_bias,
           ssd_a_log, ssd_d, ssd_norm_w, swa_sink, na_rpb, ret_decay, w_out, norm2_w, w_up, w_down,
           final_norm_w):
    d = D_MODEL
    bp, tp = x_prompt.shape[:2]
    bs, ts = x_sample.shape[:2]
    groups = [Group(0, 0, bp, tp), Group(bp * tp, bp * N_META, bs, ts)]
    n_real = bp * tp + bs * ts
    n_meta = (bp + bs) * N_META
    xr = jnp.concatenate([x_prompt.reshape(bp * tp, d), x_sample.reshape(bs * ts, d)], axis=0)
    xm = jnp.tile(meta_tokens.astype(F32), (bp + bs, 1))
    tm_norm_real = TM_NORM

    for i in range(DEPTH):
        last = i == DEPTH - 1
        lw = _layer_weights(i, w_in, ssd_conv_w, ssd_conv_b, ssd_dt_bias, ssd_a_log, ssd_d,
                            ssd_norm_w, swa_sink, na_rpb, ret_decay, w_out, w_up, w_down)
        nr = rmsnorm(xr, norm1_w[i], BF16, tm_norm_real)
        nm = rmsnorm(xm, norm1_w[i], BF16, n_meta)
        pr = matmul(nr, lw["w_main"], tm=TM_REAL, tn=TN, tk=d, out_dtype=BF16)
        pm = matmul(nm, lw["w_main"], tm=n_meta, tn=TN, tk=d, out_dtype=BF16)
        dtr = matmul(nr, lw["w_dt"], tm=TM_REAL, tn=LANE, tk=d, out_dtype=F32)
        dtm = matmul(nm, lw["w_dt"], tm=n_meta, tn=LANE, tk=d, out_dtype=F32)
        real_rows, meta_rows = [], []
        for grp in groups:
            real, meta = mix_group(pr, pm, dtr, dtm, lw, grp, need_meta=not last)
            real_rows.append(jnp.concatenate(real, axis=1))
            if meta is not None:
                meta_rows.append(jnp.concatenate(meta, axis=1))
        mixed_r = jnp.concatenate(real_rows, axis=0)
        xr = matmul(mixed_r, lw["w_out"], tm=TM_REAL, tn=TN, tk=d, out_dtype=F32,
                    epilogue="residual", residual=xr)
        n2 = rmsnorm(xr, norm2_w[i], BF16, tm_norm_real)
        u = matmul(n2, lw["w_up"], tm=TM_REAL, tn=TN, tk=d, out_dtype=BF16, epilogue="relu2")
        xr = matmul(u, lw["w_down"], tm=TM_REAL, tn=TN, tk=TK_DOWN, out_dtype=F32,
                    epilogue="residual", residual=xr)
        if not last:
            mixed_m = jnp.concatenate(meta_rows, axis=0)
            xm = matmul(mixed_m, lw["w_out"], tm=n_meta, tn=TN, tk=d, out_dtype=F32,
                        epilogue="residual", residual=xm)
            n2m = rmsnorm(xm, norm2_w[i], BF16, n_meta)
            um = matmul(n2m, lw["w_up"], tm=n_meta, tn=TN, tk=d, out_dtype=BF16, epilogue="relu2")
            xm = matmul(um, lw["w_down"], tm=n_meta, tn=TN, tk=TK_DOWN, out_dtype=F32,
                        epilogue="residual", residual=xm)

    out = rmsnorm(xr, final_norm_w, F32, tm_norm_real)
    y_prompt = out[:bp * tp].reshape(bp, tp, d)
    y_sample = out[bp * tp:].reshape(bs, ts, d)
    return (y_prompt, y_sample)
```

```python
import functools
import numpy as np
import jax
import jax.numpy as jnp
from jax import lax
from jax.experimental import pallas as pl
from jax.experimental.pallas import tpu as pltpu

F32 = jnp.float32
BF16 = jnp.bfloat16
HIGHEST = lax.Precision.HIGHEST

D_MODEL = 4096
DEPTH = 2
N_META = 16
GRID_W = 64
GROUP_WIDTH = D_MODEL // 4
D_FF = 4 * D_MODEL
EPS = 1e-6
CHUNK = 128
META_PAD = CHUNK - N_META

SSD_HEAD_DIM = 64
SSD_HEADS = 16
SSD_GROUPS = 2
SSD_HPG = 8
SSD_STATE = 128
SSD_CONV_W = 5
D_INNER = GROUP_WIDTH
SSD_CONV_DIM = D_INNER + 2 * SSD_GROUPS * SSD_STATE
SSD_GW = SSD_HPG * SSD_HEAD_DIM
SSD_CONV_TILE = 256
SSD_TOT_W = 128 + SSD_GW

SWA_HEAD_DIM = 128
SWA_HEADS = 8
SWA_KV_HEADS = 2
SWA_REP = SWA_HEADS // SWA_KV_HEADS
SWA_WINDOW = 128
SWA_BLOCK = 128

NA_HEAD_DIM = 64
NA_HEADS = 16
NA_KH = 8
NA_KW = 16
NA_QROWS = 4
NA_QB = NA_QROWS * GRID_W
NA_KEYS = 3 * NA_QB + CHUNK

RET_HEADS = 8
RET_K_DIM = 64
RET_V_DIM = 128

LANE = 128
COL_XBC = 0
COL_Z = 1536
COL_SWA_Q = 2560
COL_SWA_K = 3584
COL_SWA_V = 3840
COL_NA_Q = 4096
COL_NA_K = 5120
COL_NA_V = 6144
COL_RET_Q = 7168
COL_RET_K = 7680
COL_RET_V = 8192
COL_RET_G = 9216
P_COLS = 10240

NEG = -1e30
VMEM_LIMIT_BYTES = 56 * 1024 * 1024


def _params(sem):
    return pltpu.CompilerParams(dimension_semantics=sem, vmem_limit_bytes=VMEM_LIMIT_BYTES)


def _nt(a, b):
    return lax.dot_general(a, b, (((1,), (1,)), ((), ())), preferred_element_type=F32)


def _tn(a, b):
    return lax.dot_general(a, b, (((0,), (0,)), ((), ())), preferred_element_type=F32)


def _dot(a, b):
    return jnp.dot(a, b, preferred_element_type=F32)


def _dot_exact(a, b):
    return jnp.dot(a, b, preferred_element_type=F32, precision=HIGHEST)


def _silu(x):
    return x * jax.nn.sigmoid(x)


def _softplus(x):
    return jnp.maximum(x, 0.0) + jnp.log1p(jnp.exp(-jnp.abs(x)))


def _iota(shape, dim):
    return lax.broadcasted_iota(jnp.int32, shape, dim)


def _rmsnorm_kernel(x_ref, w_ref, o_ref):
    x = x_ref[...]
    ms = jnp.mean(x * x, axis=-1, keepdims=True)
    o_ref[...] = (x * lax.rsqrt(ms + EPS) * w_ref[...]).astype(o_ref.dtype)


def rmsnorm(x, w, out_dtype, tm):
    m, d = x.shape
    return pl.pallas_call(
        _rmsnorm_kernel,
        grid=(m // tm,),
        in_specs=[pl.BlockSpec((tm, d), lambda i: (i, 0)),
                  pl.BlockSpec((1, d), lambda i: (0, 0))],
        out_specs=pl.BlockSpec((tm, d), lambda i: (i, 0)),
        out_shape=jax.ShapeDtypeStruct((m, d), out_dtype),
        compiler_params=_params(("parallel",)),
        name="rmsnorm",
    )(x, w.reshape(1, d).astype(F32))


def _mm_kernel(*refs, n_in, epilogue, nk):
    x_refs, w_refs, rest = refs[:n_in], refs[n_in:2 * n_in], refs[2 * n_in:]
    if epilogue == "residual":
        r_ref, o_ref = rest
    else:
        (o_ref,) = rest
    part = _dot(x_refs[0][...], w_refs[0][...])
    for x_ref, w_ref in zip(x_refs[1:], w_refs[1:]):
        part = part + _dot(x_ref[...], w_ref[...])
    if nk == 1:
        if epilogue == "relu2":
            part = jnp.square(jnp.maximum(part, 0.0))
        elif epilogue == "residual":
            part = part + r_ref[...]
        o_ref[...] = part.astype(o_ref.dtype)
    else:
        @pl.when(pl.program_id(2) == 0)
        def _():
            o_ref[...] = r_ref[...]

        o_ref[...] += part


def matmul(xs, w, *, tm, tn, tk, out_dtype, epilogue="none", residual=None, alias=True):
    xs = list(xs) if isinstance(xs, (list, tuple)) else [xs]
    n_in = len(xs)
    m, kc = xs[0].shape
    n = w.shape[1]
    if n_in > 1:
        assert tk == kc and w.shape[0] == n_in * kc
        nk = 1
        w_specs = [pl.BlockSpec((kc, tn), functools.partial(lambda i, j, k, c: (c, j), c=c))
                   for c in range(n_in)]
    else:
        nk = kc // tk
        w_specs = [pl.BlockSpec((tk, tn), lambda i, j, k: (k, j))]
    assert nk == 1 or (epilogue == "residual" and out_dtype == F32)
    in_specs = [pl.BlockSpec((tm, tk), lambda i, j, k: (i, k)) for _ in xs] + w_specs
    args = xs + [w] * n_in
    aliases = {}
    if epilogue == "residual":
        in_specs.append(pl.BlockSpec((tm, tn), lambda i, j, k: (i, j)))
        args.append(residual)
        if alias:
            aliases = {2 * n_in: 0}
    return pl.pallas_call(
        functools.partial(_mm_kernel, n_in=n_in, epilogue=epilogue, nk=nk),
        grid=(m // tm, n // tn, nk),
        in_specs=in_specs,
        out_specs=pl.BlockSpec((tm, tn), lambda i, j, k: (i, j)),
        out_shape=jax.ShapeDtypeStruct((m, n), out_dtype),
        input_output_aliases=aliases,
        compiler_params=_params(("parallel", "parallel", "arbitrary")),
        name="matmul_" + epilogue,
    )(*args)


class Group:
    def __init__(self, mbase, bsz, t):
        self.mbase = mbase
        self.bsz = bsz
        self.t = t
        self.nc = t // CHUNK
        self.nc1 = self.nc + 1
        self.lp = self.nc1 * CHUNK

    def rblk(self, rows):
        per = self.t // rows
        return lambda b: b * per

    def mblk(self):
        first = self.mbase // N_META
        return lambda b: first + b


def _meta_front(m_val, width, dtype):
    return jnp.concatenate([jnp.zeros((META_PAD, width), dtype), m_val], axis=0)


def _ssd_pre_kernel(cur_ref, prev_ref, next_ref, meta_ref, dtr_ref, dtm_ref, cw_ref, cb_ref,
                    bias4_ref, alog4_ref,
                    act_ref, cs_ref, dtv_ref, cst_ref, dtt_ref, tot_ref, ext_ref, *, nc):
    c = pl.program_id(1)
    is_meta = c == 0
    half = (SSD_CONV_W - 1) // 2
    row = _iota((CHUNK, 1), 0)
    valid = jnp.logical_or(c > 0, row >= META_PAD)
    ext_ref[0:N_META, :] = prev_ref[...].astype(F32)
    ext_ref[N_META:N_META + CHUNK, :] = cur_ref[...].astype(F32)
    ext_ref[N_META + CHUNK:, :] = next_ref[...].astype(F32)

    @pl.when(is_meta)
    def _():
        ext_ref[0:N_META + META_PAD, :] = jnp.zeros((N_META + META_PAD, SSD_CONV_DIM), F32)
        ext_ref[N_META + META_PAD:N_META + CHUNK, :] = meta_ref[...].astype(F32)

    @pl.when(c == 1)
    def _():
        ext_ref[0:N_META, :] = meta_ref[...].astype(F32)

    @pl.when(c == nc)
    def _():
        ext_ref[N_META + CHUNK:, :] = jnp.zeros((N_META, SSD_CONV_DIM), F32)

    for c0 in range(0, SSD_CONV_DIM, SSD_CONV_TILE):
        cols = slice(c0, c0 + SSD_CONV_TILE)
        acc = jnp.broadcast_to(cb_ref[:, cols], (CHUNK, SSD_CONV_TILE))
        for j in range(SSD_CONV_W):
            acc = acc + cw_ref[j:j + 1, cols] * ext_ref[pl.ds(N_META - half + j, CHUNK), cols]
        act_ref[:, cols] = jnp.where(valid, _silu(acc), 0.0).astype(act_ref.dtype)

    dtx = jnp.where(is_meta, _meta_front(dtm_ref[...], LANE, F32), dtr_ref[...])
    lane = _iota((CHUNK, LANE), 1)
    keep = jnp.logical_and(valid, lane < SSD_HPG)
    ti = _iota((CHUNK, CHUNK), 0)
    si = _iota((CHUNK, CHUNK), 1)
    tri = (si <= ti).astype(F32)
    widen = (_iota((LANE, SSD_GW), 0) == _iota((LANE, SSD_GW), 1) // SSD_HEAD_DIM).astype(F32)
    for k in range(2 * SSD_GROUPS):
        x = dtx if k == 0 else pltpu.roll(dtx, LANE - SSD_HPG * k, axis=1)
        dt = jnp.where(keep, _softplus(x + bias4_ref[k]), 0.0)
        la = dt * (-jnp.exp(alog4_ref[k]))
        incl = _dot_exact(tri, la)
        cs = incl if k < SSD_GROUPS else incl - la
        cs_ref[k] = cs
        dtv_ref[k] = dt
        cst_ref[k] = jnp.transpose(cs)[0:SSD_HPG, :]
        dtt_ref[k] = jnp.transpose(dt)[0:SSD_HPG, :]
        last = jnp.broadcast_to(incl[CHUNK - 1:CHUNK, :], (8, LANE))
        tot_ref[k] = jnp.concatenate([last, _dot_exact(last, widen)], axis=1)


def ssd_pre(pr, pm, dtr, dtm, conv_w, conv_b, bias4, alog4, grp):
    nc, nc1, bsz = grp.nc, grp.nc1, grp.bsz
    w = SSD_CONV_DIM
    r128, r16, mb = grp.rblk(CHUNK), grp.rblk(N_META), grp.mblk()
    per16 = CHUNK // N_META
    ng = 2 * SSD_GROUPS
    in_specs = [
        pl.BlockSpec((CHUNK, w), lambda b, c: (r128(b) + jnp.maximum(c - 1, 0), 0)),
        pl.BlockSpec((N_META, w), lambda b, c: (r16(b) + jnp.maximum((c - 1) * per16 - 1, 0), 0)),
        pl.BlockSpec((N_META, w), lambda b, c: (r16(b) + jnp.minimum(c * per16, nc * per16 - 1), 0)),
        pl.BlockSpec((N_META, w), lambda b, c: (mb(b), 0)),
        pl.BlockSpec((CHUNK, LANE), lambda b, c: (r128(b) + jnp.maximum(c - 1, 0), 0)),
        pl.BlockSpec((N_META, LANE), lambda b, c: (mb(b), 0)),
        pl.BlockSpec((8, w), lambda b, c: (0, 0)),
        pl.BlockSpec((1, w), lambda b, c: (0, 0)),
        pl.BlockSpec((ng, 1, LANE), lambda b, c: (0, 0, 0)),
        pl.BlockSpec((ng, 1, LANE), lambda b, c: (0, 0, 0)),
    ]
    out_specs = [
        pl.BlockSpec((None, CHUNK, w), lambda b, c: (b, c, 0)),
        pl.BlockSpec((None, ng, CHUNK, LANE), lambda b, c: (b, 0, c, 0)),
        pl.BlockSpec((None, ng, CHUNK, LANE), lambda b, c: (b, 0, c, 0)),
        pl.BlockSpec((None, ng, None, SSD_HPG, LANE), lambda b, c: (b, 0, c, 0, 0)),
        pl.BlockSpec((None, ng, None, SSD_HPG, LANE), lambda b, c: (b, 0, c, 0, 0)),
        pl.BlockSpec((None, ng, None, 8, SSD_TOT_W), lambda b, c: (b, 0, c, 0, 0)),
    ]
    out_shape = [
        jax.ShapeDtypeStruct((bsz, grp.lp, w), BF16),
        jax.ShapeDtypeStruct((bsz, ng, grp.lp, LANE), F32),
        jax.ShapeDtypeStruct((bsz, ng, grp.lp, LANE), F32),
        jax.ShapeDtypeStruct((bsz, ng, nc1, SSD_HPG, LANE), F32),
        jax.ShapeDtypeStruct((bsz, ng, nc1, SSD_HPG, LANE), F32),
        jax.ShapeDtypeStruct((bsz, ng, nc1, 8, SSD_TOT_W), F32),
    ]
    return pl.pallas_call(
        functools.partial(_ssd_pre_kernel, nc=nc),
        grid=(bsz, nc1),
        in_specs=in_specs, out_specs=out_specs, out_shape=out_shape,
        scratch_shapes=[pltpu.VMEM((CHUNK + 2 * N_META, w), F32)],
        compiler_params=_params(("parallel", "arbitrary")),
        name="ssd_pre",
    )(pr, pr, pr, pm, dtr, dtm, conv_w, conv_b, bias4, alog4)


def _ssd_scan_kernel(*refs, rev, final):
    if final:
        (xs_ref, b_ref, c_ref, cs_ref, dtv_ref, cst_ref, dtt_ref, tot_ref,
         yb_ref, zr_ref, zm_ref, dsk_ref, nw_ref, o_ref, om_ref, s_ref) = refs
    else:
        (xs_ref, b_ref, c_ref, cs_ref, dtv_ref, cst_ref, dtt_ref, tot_ref, o_ref, s_ref) = refs
    step = pl.program_id(2)

    @pl.when(step == 0)
    def _():
        s_ref[...] = jnp.zeros_like(s_ref)

    xs = xs_ref[...]
    bm = b_ref[...]
    cm = c_ref[...]
    a = cs_ref[...]
    dt = dtv_ref[...]
    a_t = cst_ref[...]
    dt_t = dtt_ref[...]
    tot = tot_ref[0:1, 0:LANE]
    decx = jnp.exp(tot_ref[0:1, LANE:])

    g = _nt(cm, bm)
    ti = _iota((CHUNK, CHUNK), 0)
    si = _iota((CHUNK, CHUNK), 1)
    mask = (si > ti) if rev else (si <= ti)
    lane = _iota((CHUNK, LANE), 1)
    lo_half = lane < SSD_HEAD_DIM
    zero = jnp.zeros((), xs.dtype)
    if rev:
        e1 = jnp.exp(tot - a)
        wst = dt * jnp.exp(a)
    else:
        e1 = jnp.exp(a)
        wst = dt * jnp.exp(tot - a)
    cm32 = cm.astype(F32)
    state = s_ref[...]
    sb = state.astype(BF16)
    pairs, wtiles = [], []
    for p in range(SSD_HPG // 2):
        xp = xs[:, p * LANE:(p + 1) * LANE]
        sp = sb[:, p * LANE:(p + 1) * LANE]
        lhs, rhs = [], []
        for q in range(2):
            h = 2 * p + q
            acol = a[:, h:h + 1]
            arow = a_t[h:h + 1, :]
            e = (arow - acol) if rev else (acol - arow)
            wm = g * jnp.exp(jnp.where(mask, e, NEG)) * dt_t[h:h + 1, :]
            mine = jnp.logical_not(lo_half) if q else lo_half
            lhs += [wm.astype(BF16), (cm32 * e1[:, h:h + 1]).astype(BF16)]
            rhs += [jnp.where(mine, xp, zero), jnp.where(mine, sp, zero)]
        pairs.append(_dot(jnp.concatenate(lhs, axis=1), jnp.concatenate(rhs, axis=0)))
        wtiles.append(jnp.where(lo_half, wst[:, 2 * p:2 * p + 1], wst[:, 2 * p + 1:2 * p + 2]))
    y = jnp.concatenate(pairs, axis=1)
    wx = jnp.concatenate(wtiles, axis=1)
    xw = (xs.astype(F32) * wx).astype(BF16)
    s_ref[...] = decx * state + _tn(bm, xw)

    if final:
        z = jnp.where(step == 0, _meta_front(zm_ref[...], SSD_GW, BF16), zr_ref[...]).astype(F32)
        yt = y + yb_ref[...] + xs.astype(F32) * dsk_ref[...]
        yt = yt * _silu(z)
        ms = jnp.mean(yt * yt, axis=-1, keepdims=True)
        res = (yt * lax.rsqrt(ms + EPS) * nw_ref[...]).astype(o_ref.dtype)

        @pl.when(step == 0)
        def _():
            om_ref[...] = res[META_PAD:, :]

        @pl.when(step > 0)
        def _():
            o_ref[...] = res
    else:
        o_ref[...] = y


def ssd_scan(act, cs4, dt4, cst4, dtt4, tot4, grp, *, rev, ybwd=None, pr=None, pm=None,
             d_skip=None, norm_w=None):
    final = not rev
    nc1, bsz = grp.nc1, grp.bsz
    d = 1 if rev else 0
    cc = (lambda c: nc1 - 1 - c) if rev else (lambda c: c)
    xs_blk = COL_XBC // SSD_GW
    b_blk = (COL_XBC + D_INNER) // LANE
    c_blk = b_blk + SSD_GROUPS
    in_specs = [
        pl.BlockSpec((None, CHUNK, SSD_GW), lambda b, g, c: (b, cc(c), xs_blk + g)),
        pl.BlockSpec((None, CHUNK, LANE), lambda b, g, c: (b, cc(c), b_blk + g)),
        pl.BlockSpec((None, CHUNK, LANE), lambda b, g, c: (b, cc(c), c_blk + g)),
        pl.BlockSpec((None, None, CHUNK, LANE), lambda b, g, c: (b, d * SSD_GROUPS + g, cc(c), 0)),
        pl.BlockSpec((None, None, CHUNK, LANE), lambda b, g, c: (b, d * SSD_GROUPS + g, cc(c), 0)),
        pl.BlockSpec((None, None, None, SSD_HPG, LANE), lambda b, g, c: (b, d * SSD_GROUPS + g, cc(c), 0, 0)),
        pl.BlockSpec((None, None, None, SSD_HPG, LANE), lambda b, g, c: (b, d * SSD_GROUPS + g, cc(c), 0, 0)),
        pl.BlockSpec((None, None, None, 8, SSD_TOT_W), lambda b, g, c: (b, d * SSD_GROUPS + g, cc(c), 0, 0)),
    ]
    args = [act, act, act, cs4, dt4, cst4, dtt4, tot4]
    if final:
        r128, mb = grp.rblk(CHUNK), grp.mblk()
        z_blk = COL_Z // SSD_GW
        in_specs += [
            pl.BlockSpec((None, CHUNK, SSD_GW), lambda b, g, c: (b, c, g)),
            pl.BlockSpec((CHUNK, SSD_GW), lambda b, g, c: (r128(b) + jnp.maximum(c - 1, 0), z_blk + g)),
            pl.BlockSpec((N_META, SSD_GW), lambda b, g, c: (mb(b), z_blk + g)),
            pl.BlockSpec((1, SSD_GW), lambda b, g, c: (0, g)),
            pl.BlockSpec((1, SSD_GW), lambda b, g, c: (0, g)),
        ]
        args += [ybwd, pr, pm, d_skip, norm_w]
        out_specs = [pl.BlockSpec((CHUNK, SSD_GW), lambda b, g, c: (r128(b) + jnp.maximum(c - 1, 0), g)),
                     pl.BlockSpec((N_META, SSD_GW), lambda b, g, c: (b, g))]
        out_shape = [jax.ShapeDtypeStruct((bsz * grp.t, D_INNER), BF16),
                     jax.ShapeDtypeStruct((bsz * N_META, D_INNER), BF16)]
    else:
        out_specs = pl.BlockSpec((None, CHUNK, SSD_GW), lambda b, g, c: (b, cc(c), g))
        out_shape = jax.ShapeDtypeStruct((bsz, grp.lp, D_INNER), F32)
    return pl.pallas_call(
        functools.partial(_ssd_scan_kernel, rev=rev, final=final),
        grid=(bsz, SSD_GROUPS, nc1),
        in_specs=in_specs,
        out_specs=out_specs,
        out_shape=out_shape,
        scratch_shapes=[pltpu.VMEM((SSD_STATE, SSD_GW), F32)],
        compiler_params=_params(("parallel", "parallel", "arbitrary")),
        name="ssd_scan_" + ("bwd" if rev else "fwd"),
    )(*args)


def _ret_scan_kernel(*refs, rev, final, nc1):
    if final:
        (dl_ref, qr_ref, qm_ref, kr_ref, km_ref, vr_ref, vm_ref, yb_ref, gr_ref, gm_ref,
         o_ref, om_ref, s_ref, dm_ref, e1_ref, ws_ref, dec_ref) = refs
    else:
        (dl_ref, qr_ref, qm_ref, kr_ref, km_ref, vr_ref, vm_ref,
         o_ref, s_ref, dm_ref, e1_ref, ws_ref, dec_ref) = refs
    step = pl.program_id(1)
    chunk = (nc1 - 1 - step) if rev else step
    is_meta = chunk == 0

    @pl.when(step == 0)
    def _():
        s_ref[...] = jnp.zeros_like(s_ref)
        ti = _iota((CHUNK, CHUNK), 0)
        si = _iota((CHUNK, CHUNK), 1)
        tf = ti.astype(F32)
        for h in range(RET_HEADS):
            x = jnp.full((CHUNK, CHUNK), dl_ref[1 if rev else 0, h], F32)
            lg = jnp.minimum(x, 0.0) - jnp.log1p(jnp.exp(-jnp.abs(x)))
            if rev:
                dm_ref[h] = jnp.exp(jnp.where(si > ti, (si - ti).astype(F32) * lg, NEG))
                e1_ref[h] = jnp.exp((CHUNK - tf) * lg)
                ws_ref[h] = jnp.exp(tf * lg)
            else:
                dm_ref[h] = jnp.exp(jnp.where(si <= ti, (ti - si).astype(F32) * lg, NEG))
                e1_ref[h] = jnp.exp((tf + 1.0) * lg)
                ws_ref[h] = jnp.exp((CHUNK - 1.0 - tf) * lg)
            dec_ref[h] = jnp.exp(CHUNK * lg)

    def pick(r_ref, m_ref):
        width = r_ref.shape[-1]
        return jnp.where(is_meta, _meta_front(m_ref[...], width, BF16), r_ref[...])

    q = pick(qr_ref, qm_ref)
    k = pick(kr_ref, km_ref)
    v = pick(vr_ref, vm_ref)
    if final:
        gate = pick(gr_ref, gm_ref)
    lane = _iota((CHUNK, LANE), 1)
    zero = jnp.zeros((), BF16)
    scale = jnp.asarray(RET_K_DIM ** -0.5, BF16)
    heads = range(RET_HEADS)
    pairs = [slice((h // 2) * LANE, (h // 2 + 1) * LANE) for h in heads]
    cols = [slice(h * LANE, (h + 1) * LANE) for h in heads]
    qh = [jnp.where((lane >= RET_K_DIM) if h % 2 else (lane < RET_K_DIM), q[:, pairs[h]], zero) * scale
          for h in heads]
    sc = [_nt(qh[h], k[:, pairs[h]]) for h in heads]
    inter = [_dot(qh[h], s_ref[h].astype(BF16)) for h in heads]
    for h in heads:
        vw = (v[:, cols[h]].astype(F32) * ws_ref[h]).astype(BF16)
        s_ref[h] = dec_ref[h] * s_ref[h] + _tn(k[:, pairs[h]], vw)
    ys = [_dot((sc[h] * dm_ref[h]).astype(BF16), v[:, cols[h]]) + e1_ref[h] * inter[h] for h in heads]
    for h in heads:
        y = ys[h]
        if final:
            yt = y + yb_ref[:, cols[h]]
            mu = jnp.mean(yt, axis=-1, keepdims=True)
            dev = yt - mu
            var = jnp.mean(dev * dev, axis=-1, keepdims=True)
            y = dev * lax.rsqrt(var + 1e-5) * _silu(gate[:, cols[h]].astype(F32))
        o_ref[:, cols[h]] = y.astype(o_ref.dtype)

    if final:
        @pl.when(step == 0)
        def _():
            om_ref[...] = o_ref[META_PAD:, :]


def ret_scan(pr, pm, decay, grp, *, rev, ybwd=None):
    final = not rev
    nc1, bsz = grp.nc1, grp.bsz
    r128, mb = grp.rblk(CHUNK), grp.mblk()
    cc = (lambda c: nc1 - 1 - c) if rev else (lambda c: c)
    qk_w = RET_HEADS * RET_K_DIM

    def real(col0, width):
        blk = col0 // width
        return pl.BlockSpec((CHUNK, width), lambda b, c: (r128(b) + jnp.maximum(cc(c) - 1, 0), blk))

    def meta(col0, width):
        blk = col0 // width
        return pl.BlockSpec((N_META, width), lambda b, c: (mb(b), blk))

    in_specs = [pl.BlockSpec(memory_space=pltpu.SMEM),
                real(COL_RET_Q, qk_w), meta(COL_RET_Q, qk_w),
                real(COL_RET_K, qk_w), meta(COL_RET_K, qk_w),
                real(COL_RET_V, GROUP_WIDTH), meta(COL_RET_V, GROUP_WIDTH)]
    args = [decay, pr, pm, pr, pm, pr, pm]
    if final:
        in_specs += [pl.BlockSpec((None, CHUNK, GROUP_WIDTH), lambda b, c: (b, c, 0)),
                     real(COL_RET_G, GROUP_WIDTH), meta(COL_RET_G, GROUP_WIDTH)]
        args += [ybwd, pr, pm]
        out_specs = [pl.BlockSpec((CHUNK, GROUP_WIDTH), lambda b, c: (r128(b) + jnp.maximum(c - 1, 0), 0)),
                     pl.BlockSpec((N_META, GROUP_WIDTH), lambda b, c: (b, 0))]
        out_shape = [jax.ShapeDtypeStruct((bsz * grp.t, GROUP_WIDTH), BF16),
                     jax.ShapeDtypeStruct((bsz * N_META, GROUP_WIDTH), BF16)]
    else:
        out_specs = pl.BlockSpec((None, CHUNK, GROUP_WIDTH), lambda b, c: (b, cc(c), 0))
        out_shape = jax.ShapeDtypeStruct((bsz, grp.lp, GROUP_WIDTH), F32)
    table = pltpu.VMEM((RET_HEADS, CHUNK, CHUNK), F32)
    return pl.pallas_call(
        functools.partial(_ret_scan_kernel, rev=rev, final=final, nc1=nc1),
        grid=(bsz, nc1),
        in_specs=in_specs,
        out_specs=out_specs,
        out_shape=out_shape,
        scratch_shapes=[pltpu.VMEM((RET_HEADS, LANE, RET_V_DIM), F32), table, table, table, table],
        compiler_params=_params(("parallel", "arbitrary")),
        name="ret_scan_" + ("bwd" if rev else "fwd"),
    )(*args)


def _alibi_slope_rows(rows_per_head, nrows, g):
    hd = _iota((nrows, 1), 0) // rows_per_head
    s = jnp.full((nrows, 1), 2.0 ** -SWA_REP, F32)
    for r in range(SWA_REP - 1):
        s = jnp.where(hd == r, 2.0 ** -(r + 1), s)
    return s * jnp.where(g == 1, 2.0 ** -SWA_REP, 1.0)


def _sink_rows(sink_ref, rows_per_head, nrows, g):
    hd = _iota((nrows, 1), 0) // rows_per_head
    s = jnp.full((nrows, 1), sink_ref[g * SWA_REP + SWA_REP - 1], F32)
    for r in range(SWA_REP - 1):
        s = jnp.where(hd == r, sink_ref[g * SWA_REP + r], s)
    return s


def _softmax_with_sink(s, sink, vall):
    m = jnp.maximum(jnp.max(s, axis=-1, keepdims=True), sink)
    p = jnp.exp(s - m)
    denom = jnp.sum(p, axis=-1, keepdims=True) + jnp.exp(sink - m)
    return _dot(p.astype(BF16), vall) / denom


def _swa_kernel(sink_ref, q_ref, kp_ref, ko_ref, kn_ref, vp_ref, vo_ref, vn_ref, km_ref, vm_ref,
                o_ref, bias_ref, *, nb):
    g = pl.program_id(1)
    n = pl.program_id(2)
    blk = SWA_BLOCK
    nq = SWA_REP * blk
    nkeys = 4 * blk

    @pl.when(n == 0)
    def _():
        t = _iota((nq, nkeys), 0) % blk
        col = _iota((nq, nkeys), 1)
        rel = col - blk - t
        dist = jnp.abs(rel)
        band = jnp.logical_and(col < 3 * blk, dist <= SWA_WINDOW)
        is_meta = jnp.logical_and(col >= 3 * blk, col < 3 * blk + N_META)
        slope = _alibi_slope_rows(blk, nq, g)
        bias_ref[...] = jnp.where(band, -slope * dist.astype(F32), jnp.where(is_meta, 0.0, NEG))

    q = q_ref[...]
    qs = jnp.concatenate([q[:, r * blk:(r + 1) * blk] for r in range(SWA_REP)], axis=0)
    pad = jnp.zeros((META_PAD, SWA_HEAD_DIM), BF16)
    kall = jnp.concatenate([kp_ref[...], ko_ref[...], kn_ref[...], km_ref[...], pad], axis=0)
    vall = jnp.concatenate([vp_ref[...], vo_ref[...], vn_ref[...], vm_ref[...], pad], axis=0)
    s = _nt(qs, kall) * (SWA_HEAD_DIM ** -0.5) + bias_ref[...]
    col = _iota((nq, nkeys), 1)
    lo = jnp.where(n == 0, blk, 0)
    hi = jnp.where(n == nb - 1, 2 * blk, 3 * blk)
    outside = jnp.logical_or(col < lo, jnp.logical_and(col >= hi, col < 3 * blk))
    s = jnp.where(outside, NEG, s)
    o = _softmax_with_sink(s, _sink_rows(sink_ref, blk, nq, g), vall)
    o_ref[...] = jnp.concatenate([o[r * blk:(r + 1) * blk] for r in range(SWA_REP)],
                                 axis=1).astype(o_ref.dtype)


def swa_real(pr, pm, sink, grp):
    bsz, nb = grp.bsz, grp.t // SWA_BLOCK
    r128, mb = grp.rblk(SWA_BLOCK), grp.mblk()
    qw = SWA_REP * SWA_HEAD_DIM
    q_blk, k_blk, v_blk = COL_SWA_Q // qw, COL_SWA_K // LANE, COL_SWA_V // LANE

    def kv(col_blk, off):
        return pl.BlockSpec((SWA_BLOCK, LANE),
                            lambda b, g, n: (r128(b) + jnp.clip(n + off, 0, nb - 1), col_blk + g))

    in_specs = [pl.BlockSpec(memory_space=pltpu.SMEM),
                pl.BlockSpec((SWA_BLOCK, qw), lambda b, g, n: (r128(b) + n, q_blk + g)),
                kv(k_blk, -1), kv(k_blk, 0), kv(k_blk, 1),
                kv(v_blk, -1), kv(v_blk, 0), kv(v_blk, 1),
                pl.BlockSpec((N_META, LANE), lambda b, g, n: (mb(b), k_blk + g)),
                pl.BlockSpec((N_META, LANE), lambda b, g, n: (mb(b), v_blk + g))]
    return pl.pallas_call(
        functools.partial(_swa_kernel, nb=nb),
        grid=(bsz, SWA_KV_HEADS, nb),
        in_specs=in_specs,
        out_specs=pl.BlockSpec((SWA_BLOCK, qw), lambda b, g, n: (b * nb + n, g)),
        out_shape=jax.ShapeDtypeStruct((bsz * grp.t, GROUP_WIDTH), BF16),
        scratch_shapes=[pltpu.VMEM((SWA_REP * SWA_BLOCK, 4 * SWA_BLOCK), F32)],
        compiler_params=_params(("parallel", "parallel", "arbitrary")),
        name="swa_real",
    )(sink, pr, pr, pr, pr, pr, pr, pr, pm, pm)


def _swa_meta_kernel(sink_ref, q_ref, k0_ref, v0_ref, km_ref, vm_ref, o_ref):
    g = pl.program_id(1)
    nq = SWA_REP * N_META
    nkeys = 2 * SWA_BLOCK
    q = q_ref[...]
    qs = jnp.concatenate([q[:, r * LANE:(r + 1) * LANE] for r in range(SWA_REP)], axis=0)
    pad = jnp.zeros((META_PAD, SWA_HEAD_DIM), BF16)
    kall = jnp.concatenate([k0_ref[...], km_ref[...], pad], axis=0)
    vall = jnp.concatenate([v0_ref[...], vm_ref[...], pad], axis=0)
    i = _iota((nq, nkeys), 0) % N_META
    col = _iota((nq, nkeys), 1)
    mdist = N_META + col - i
    near = jnp.logical_and(col < SWA_BLOCK, mdist <= SWA_WINDOW)
    is_meta = jnp.logical_and(col >= SWA_BLOCK, col < SWA_BLOCK + N_META)
    slope = _alibi_slope_rows(N_META, nq, g)
    bias = jnp.where(near, -slope * mdist.astype(F32), jnp.where(is_meta, 0.0, NEG))
    s = _nt(qs, kall) * (SWA_HEAD_DIM ** -0.5) + bias
    o = _softmax_with_sink(s, _sink_rows(sink_ref, N_META, nq, g), vall)
    o_ref[...] = jnp.concatenate([o[r * N_META:(r + 1) * N_META] for r in range(SWA_REP)],
                                 axis=1).astype(o_ref.dtype)


def swa_meta(pr, pm, sink, grp):
    bsz = grp.bsz
    r128, mb = grp.rblk(SWA_BLOCK), grp.mblk()
    qw = SWA_REP * SWA_HEAD_DIM
    q_blk, k_blk, v_blk = COL_SWA_Q // qw, COL_SWA_K // LANE, COL_SWA_V // LANE
    in_specs = [pl.BlockSpec(memory_space=pltpu.SMEM),
                pl.BlockSpec((N_META, qw), lambda b, g: (mb(b), q_blk + g)),
                pl.BlockSpec((SWA_BLOCK, LANE), lambda b, g: (r128(b), k_blk + g)),
                pl.BlockSpec((SWA_BLOCK, LANE), lambda b, g: (r128(b), v_blk + g)),
                pl.BlockSpec((N_META, LANE), lambda b, g: (mb(b), k_blk + g)),
                pl.BlockSpec((N_META, LANE), lambda b, g: (mb(b), v_blk + g))]
    return pl.pallas_call(
        _swa_meta_kernel,
        grid=(bsz, SWA_KV_HEADS),
        in_specs=in_specs,
        out_specs=pl.BlockSpec((N_META, qw), lambda b, g: (b, g)),
        out_shape=jax.ShapeDtypeStruct((bsz * N_META, GROUP_WIDTH), BF16),
        compiler_params=_params(("parallel", "parallel")),
        name="swa_meta",
    )(sink, pm, pr, pr, pm, pm)


NA_KROWS = 3 * NA_QROWS
NA_RPB_H = 2 * NA_KH - 1
NA_RPB_W = 2 * NA_KW - 1
NA_VARIANTS = 3
NA_SM_ROWS = 32


def _na_row_ok(variant, i, j):
    if variant == 0:
        return NA_QROWS <= j < NA_QROWS + NA_KH
    if variant == 1:
        return i <= j < i + NA_KH
    return j < NA_KH


def _na_bias_kernel(rpb_ref, o_ref):
    h = pl.program_id(0)
    base = h * (NA_RPB_H * NA_RPB_W)
    qc = _iota((GRID_W, LANE), 0)
    lane = _iota((GRID_W, LANE), 1)
    lo_half = lane < GRID_W
    kc = jnp.where(lo_half, lane, lane - GRID_W)
    d = kc - qc + (NA_KW - 1)
    col_start = jnp.clip(qc - NA_KW // 2, 0, GRID_W - NA_KW)
    col_ok = jnp.logical_and(kc >= col_start, kc < col_start + NA_KW)
    neg = jnp.full((GRID_W, LANE), NEG, F32)
    for a in range(NA_RPB_H - 1):
        val = jnp.zeros((GRID_W, LANE), F32)
        for b in range(NA_RPB_W):
            s_lo = rpb_ref[base + a * NA_RPB_W + b]
            s_hi = rpb_ref[base + (a + 1) * NA_RPB_W + b]
            val = jnp.where(d == b, jnp.where(lo_half, s_lo, s_hi), val)
        tile = jnp.where(col_ok, val, neg)
        for jp in range(NA_KROWS // 2):
            i = 2 * jp + (NA_QROWS - 1) - a
            if not 0 <= i < NA_QROWS:
                continue
            for v in range(NA_VARIANTS):
                ok_lo, ok_hi = _na_row_ok(v, i, 2 * jp), _na_row_ok(v, i, 2 * jp + 1)
                if ok_lo and ok_hi:
                    blk = tile
                elif ok_lo:
                    blk = jnp.where(lo_half, tile, neg)
                elif ok_hi:
                    blk = jnp.where(lo_half, neg, tile)
                else:
                    blk = neg
                o_ref[v, i * GRID_W:(i + 1) * GRID_W, jp * LANE:(jp + 1) * LANE] = blk
    meta_cols = jnp.where(_iota((NA_QB, LANE), 1) < N_META, 0.0, NEG)
    for v in range(NA_VARIANTS):
        o_ref[v, :, 3 * NA_QB:] = meta_cols


def na_bias(rpb):
    return pl.pallas_call(
        _na_bias_kernel,
        grid=(NA_HEADS,),
        in_specs=[pl.BlockSpec(memory_space=pltpu.SMEM)],
        out_specs=pl.BlockSpec((NA_VARIANTS, None, NA_QB, NA_KEYS), lambda h: (0, h, 0, 0)),
        out_shape=jax.ShapeDtypeStruct((NA_VARIANTS, NA_HEADS, NA_QB, NA_KEYS), F32),
        compiler_params=_params(("parallel",)),
        name="na_bias",
    )(rpb.astype(F32).reshape(-1))


def _na_kernel(q_ref, kp_ref, ko_ref, kn_ref, vp_ref, vo_ref, vn_ref, km_ref, vm_ref,
               bias_ref, o_ref, s_ref, p_ref, l_ref):
    q = q_ref[...] * jnp.asarray(NA_HEAD_DIM ** -0.5, BF16)
    pad = jnp.zeros((META_PAD, LANE), BF16)
    kall = jnp.concatenate([kp_ref[...], ko_ref[...], kn_ref[...], km_ref[...], pad], axis=0)
    vall = jnp.concatenate([vp_ref[...], vo_ref[...], vn_ref[...], vm_ref[...], pad], axis=0)
    zero = jnp.zeros((), BF16)
    qlane = _iota((NA_QB, LANE), 1)
    vlane = _iota((NA_KEYS, LANE), 1)
    for hh in range(2):
        qsel = (qlane >= NA_HEAD_DIM) if hh else (qlane < NA_HEAD_DIM)
        s_ref[hh] = _nt(jnp.where(qsel, q, zero), kall)
    out = None
    for hh in range(2):
        for c in range(NA_QB // NA_SM_ROWS):
            rows = slice(c * NA_SM_ROWS, (c + 1) * NA_SM_ROWS)
            s = s_ref[hh, rows, :] + bias_ref[hh, rows, :]
            m = jnp.max(s, axis=-1, keepdims=True)
            p = jnp.exp(s - m)
            p_ref[hh, rows, :] = p.astype(BF16)
            l_ref[hh, rows, :] = jnp.broadcast_to(1.0 / jnp.sum(p, axis=-1, keepdims=True),
                                                  (NA_SM_ROWS, LANE))
        vsel = (vlane >= NA_HEAD_DIM) if hh else (vlane < NA_HEAD_DIM)
        o = _dot(p_ref[hh], jnp.where(vsel, vall, zero)) * l_ref[hh]
        out = o if out is None else out + o
    o_ref[...] = out.astype(o_ref.dtype)


def na_real(pr, pm, bias, grp):
    bsz, nqb = grp.bsz, grp.t // NA_QB
    rq, mb = grp.rblk(NA_QB), grp.mblk()
    q_blk, k_blk, v_blk = COL_NA_Q // LANE, COL_NA_K // LANE, COL_NA_V // LANE
    npair = NA_HEADS // 2

    def kv(col_blk, off):
        return pl.BlockSpec((NA_QB, LANE),
                            lambda p, b, n: (rq(b) + jnp.clip(n + off, 0, nqb - 1), col_blk + p))

    in_specs = [pl.BlockSpec((NA_QB, LANE), lambda p, b, n: (rq(b) + n, q_blk + p)),
                kv(k_blk, -1), kv(k_blk, 0), kv(k_blk, 1),
                kv(v_blk, -1), kv(v_blk, 0), kv(v_blk, 1),
                pl.BlockSpec((N_META, LANE), lambda p, b, n: (mb(b), k_blk + p)),
                pl.BlockSpec((N_META, LANE), lambda p, b, n: (mb(b), v_blk + p)),
                pl.BlockSpec((None, 2, NA_QB, NA_KEYS),
                             lambda p, b, n: (jnp.where(n == 0, 0, jnp.where(n == nqb - 1, 2, 1)), p, 0, 0))]
    return pl.pallas_call(
        _na_kernel,
        grid=(npair, bsz, nqb),
        in_specs=in_specs,
        out_specs=pl.BlockSpec((NA_QB, LANE), lambda p, b, n: (b * nqb + n, p)),
        out_shape=jax.ShapeDtypeStruct((bsz * grp.t, GROUP_WIDTH), BF16),
        scratch_shapes=[pltpu.VMEM((2, NA_QB, NA_KEYS), F32), pltpu.VMEM((2, NA_QB, NA_KEYS), BF16),
                        pltpu.VMEM((2, NA_QB, LANE), F32)],
        compiler_params=_params(("parallel", "parallel", "arbitrary")),
        name="na_real",
    )(pr, pr, pr, pr, pr, pr, pr, pm, pm, bias)


def _na_meta_kernel(q_ref, k_ref, v_ref, km_ref, vm_ref, o_ref):
    nwin = NA_KH * GRID_W
    nkeys = nwin + CHUNK
    q = q_ref[...]
    pad = jnp.zeros((META_PAD, LANE), BF16)
    kall = jnp.concatenate([k_ref[...], km_ref[...], pad], axis=0)
    vall = jnp.concatenate([v_ref[...], vm_ref[...], pad], axis=0)
    col = _iota((N_META, nkeys), 1)
    in_win = jnp.logical_and(col < nwin, col % GRID_W < NA_KW)
    is_meta = jnp.logical_and(col >= nwin, col < nwin + N_META)
    bias = jnp.where(jnp.logical_or(in_win, is_meta), 0.0, NEG)
    zero = jnp.zeros((), BF16)
    qlane = _iota((N_META, LANE), 1)
    vlane = _iota((nkeys, LANE), 1)
    out = jnp.zeros((N_META, LANE), F32)
    for hh in range(2):
        qsel = (qlane >= NA_HEAD_DIM) if hh else (qlane < NA_HEAD_DIM)
        vsel = (vlane >= NA_HEAD_DIM) if hh else (vlane < NA_HEAD_DIM)
        s = _nt(jnp.where(qsel, q, zero), kall) * (NA_HEAD_DIM ** -0.5) + bias
        m = jnp.max(s, axis=-1, keepdims=True)
        p = jnp.exp(s - m)
        denom = jnp.sum(p, axis=-1, keepdims=True)
        out = out + _dot(p.astype(BF16), jnp.where(vsel, vall, zero)) / denom
    o_ref[...] = out.astype(o_ref.dtype)


def na_meta(pr, pm, grp):
    bsz = grp.bsz
    nwin = NA_KH * GRID_W
    rw, mb = grp.rblk(nwin), grp.mblk()
    q_blk, k_blk, v_blk = COL_NA_Q // LANE, COL_NA_K // LANE, COL_NA_V // LANE
    in_specs = [pl.BlockSpec((N_META, LANE), lambda b, p: (mb(b), q_blk + p)),
                pl.BlockSpec((nwin, LANE), lambda b, p: (rw(b), k_blk + p)),
                pl.BlockSpec((nwin, LANE), lambda b, p: (rw(b), v_blk + p)),
                pl.BlockSpec((N_META, LANE), lambda b, p: (mb(b), k_blk + p)),
                pl.BlockSpec((N_META, LANE), lambda b, p: (mb(b), v_blk + p))]
    return pl.pallas_call(
        _na_meta_kernel,
        grid=(bsz, NA_HEADS // 2),
        in_specs=in_specs,
        out_specs=pl.BlockSpec((N_META, LANE), lambda b, p: (b, p)),
        out_shape=jax.ShapeDtypeStruct((bsz * N_META, GROUP_WIDTH), BF16),
        compiler_params=_params(("parallel", "parallel")),
        name="na_meta",
    )(pm, pr, pr, pm, pm)


def mix_group(pr, pm, dtr, dtm, lw, grp, need_meta):
    act, cs4, dt4, cst4, dtt4, tot4 = ssd_pre(pr, pm, dtr, dtm, lw["conv_w"], lw["conv_b"],
                                               lw["bias4"], lw["alog4"], grp)
    yb = ssd_scan(act, cs4, dt4, cst4, dtt4, tot4, grp, rev=True)
    y_ssd, ym_ssd = ssd_scan(act, cs4, dt4, cst4, dtt4, tot4, grp, rev=False, ybwd=yb, pr=pr, pm=pm,
                             d_skip=lw["d_skip"], norm_w=lw["ssd_norm_w"])
    rb = ret_scan(pr, pm, lw["ret_decay"], grp, rev=True)
    y_ret, ym_ret = ret_scan(pr, pm, lw["ret_decay"], grp, rev=False, ybwd=rb)
    real = [y_ssd, swa_real(pr, pm, lw["sink"], grp), na_real(pr, pm, lw["na_bias"], grp), y_ret]
    meta = None
    if need_meta:
        meta = [ym_ssd, swa_meta(pr, pm, lw["sink"], grp), na_meta(pr, pm, grp), ym_ret]
    return real, meta


def _layer_weights(i, w_in, ssd_conv_w, ssd_conv_b, ssd_dt_bias, ssd_a_log, ssd_d, ssd_norm_w,
                   swa_sink, na_rpb, ret_decay, w_out, w_up, w_down):
    wi = w_in[i]
    z0, x0 = 0, D_INNER
    dt0 = D_INNER + SSD_CONV_DIM
    rest0 = dt0 + 2 * SSD_HEADS
    w_main = jnp.concatenate([wi[:, x0:dt0], wi[:, z0:x0], wi[:, rest0:]], axis=1).astype(BF16)
    w_dt = jnp.pad(wi[:, dt0:rest0], ((0, 0), (0, LANE - 2 * SSD_HEADS))).astype(BF16)

    def per_dir_group(v):
        v4 = v.astype(F32).reshape(2 * SSD_GROUPS, 1, SSD_HPG)
        return jnp.pad(v4, ((0, 0), (0, 0), (0, LANE - SSD_HPG)))

    return dict(
        w_main=w_main, w_dt=w_dt,
        conv_w=jnp.pad(ssd_conv_w[i].astype(F32), ((0, 8 - SSD_CONV_W), (0, 0))),
        conv_b=ssd_conv_b[i].astype(F32).reshape(1, SSD_CONV_DIM),
        bias4=per_dir_group(ssd_dt_bias[i]), alog4=per_dir_group(ssd_a_log[i]),
        d_skip=jnp.repeat(ssd_d[i].astype(F32), SSD_HEAD_DIM).reshape(1, D_INNER),
        ssd_norm_w=ssd_norm_w[i].astype(F32).reshape(1, D_INNER),
        sink=swa_sink[i].astype(F32), na_bias=na_bias(na_rpb[i]),
        ret_decay=ret_decay[i].astype(F32),
        w_out=w_out[i].astype(BF16), w_up=w_up[i].astype(BF16), w_down=w_down[i].astype(BF16),
    )


TM_REAL = 1024
TN = 512
TK_DOWN = 4096
TM_NORM = 256


def _mlp(x, tm, lw, norm_w):
    n2 = rmsnorm(x, norm_w, BF16, min(tm, TM_NORM))
    u = matmul(n2, lw["w_up"], tm=tm, tn=TN, tk=D_MODEL, out_dtype=BF16, epilogue="relu2")
    return matmul(u, lw["w_down"], tm=tm, tn=TN, tk=TK_DOWN, out_dtype=F32,
                  epilogue="residual", residual=x)


def kernel(x_prompt, x_sample, meta_tokens, norm1_w, w_in, ssd_conv_w, ssd_conv_b, ssd_dt_bias,
           ssd_a_log, ssd_d, ssd_norm_w, swa_sink, na_rpb, ret_decay, w_out, norm2_w, w_up, w_down,
           final_norm_w):
    d = D_MODEL
    inputs = [x_prompt, x_sample]
    groups, mbase = [], 0
    for x in inputs:
        groups.append(Group(mbase, x.shape[0], x.shape[1]))
        mbase += x.shape[0] * N_META
    n_meta = mbase
    xs = [x.reshape(-1, d).astype(F32) for x in inputs]
    xm = jnp.tile(meta_tokens.astype(F32), (n_meta // N_META, 1))

    for i in range(DEPTH):
        last = i == DEPTH - 1
        lw = _layer_weights(i, w_in, ssd_conv_w, ssd_conv_b, ssd_dt_bias, ssd_a_log, ssd_d,
                            ssd_norm_w, swa_sink, na_rpb, ret_decay, w_out, w_up, w_down)
        nm = rmsnorm(xm, norm1_w[i], BF16, n_meta)
        pm = matmul(nm, lw["w_main"], tm=n_meta, tn=TN, tk=d, out_dtype=BF16)
        dtm = matmul(nm, lw["w_dt"], tm=n_meta, tn=LANE, tk=d, out_dtype=F32)
        meta_parts = []
        for gi, grp in enumerate(groups):
            nr = rmsnorm(xs[gi], norm1_w[i], BF16, TM_NORM)
            pr = matmul(nr, lw["w_main"], tm=TM_REAL, tn=TN, tk=d, out_dtype=BF16)
            dtr = matmul(nr, lw["w_dt"], tm=TM_REAL, tn=LANE, tk=d, out_dtype=F32)
            real, meta = mix_group(pr, pm, dtr, dtm, lw, grp, need_meta=not last)
            meta_parts.append(meta)
            h = matmul(real, lw["w_out"], tm=TM_REAL, tn=TN, tk=GROUP_WIDTH, out_dtype=F32,
                       epilogue="residual", residual=xs[gi], alias=i > 0)
            xs[gi] = _mlp(h, TM_REAL, lw, norm2_w[i])
        if not last:
            mixed_m = [jnp.concatenate([m[k] for m in meta_parts], axis=0) for k in range(4)]
            hm = matmul(mixed_m, lw["w_out"], tm=n_meta, tn=TN, tk=GROUP_WIDTH, out_dtype=F32,
                        epilogue="residual", residual=xm)
            xm = _mlp(hm, n_meta, lw, norm2_w[i])

    outs = [rmsnorm(x, final_norm_w, F32, TM_NORM).reshape(inp.shape)
            for x, inp in zip(xs, inputs)]
    return tuple(outs)
```

```python
import functools
import numpy as np
import jax
import jax.numpy as jnp
from jax import lax
from jax.experimental import pallas as pl
from jax.experimental.pallas import tpu as pltpu

F32 = jnp.float32
BF16 = jnp.bfloat16
HIGHEST = lax.Precision.HIGHEST

D_MODEL = 4096
DEPTH = 2
N_META = 16
GRID_W = 64
GROUP_WIDTH = D_MODEL // 4
D_FF = 4 * D_MODEL
EPS = 1e-6
CHUNK = 128
META_PAD = CHUNK - N_META

SSD_HEAD_DIM = 64
SSD_HEADS = 16
SSD_GROUPS = 2
SSD_HPG = 8
SSD_STATE = 128
SSD_CONV_W = 5
D_INNER = GROUP_WIDTH
SSD_CONV_DIM = D_INNER + 2 * SSD_GROUPS * SSD_STATE
SSD_GW = SSD_HPG * SSD_HEAD_DIM
SSD_CONV_TILE = 256
SSD_TOT_W = 128 + SSD_GW

SWA_HEAD_DIM = 128
SWA_HEADS = 8
SWA_KV_HEADS = 2
SWA_REP = SWA_HEADS // SWA_KV_HEADS
SWA_WINDOW = 128
SWA_BLOCK = 128
SWA_SM_ROWS = 32

NA_HEAD_DIM = 64
NA_HEADS = 16
NA_KH = 8
NA_KW = 16
NA_QROWS = 4
NA_QB = NA_QROWS * GRID_W
NA_KEYS = 3 * NA_QB + CHUNK

RET_HEADS = 8
RET_K_DIM = 64
RET_V_DIM = 128

LANE = 128
COL_XBC = 0
COL_Z = 1536
COL_SWA_Q = 2560
COL_SWA_K = 3584
COL_SWA_V = 3840
COL_NA_Q = 4096
COL_NA_K = 5120
COL_NA_V = 6144
COL_RET_Q = 7168
COL_RET_K = 7680
COL_RET_V = 8192
COL_RET_G = 9216
P_COLS = 10240

NEG = -1e30
VMEM_LIMIT_BYTES = 56 * 1024 * 1024


def _params(sem):
    return pltpu.CompilerParams(dimension_semantics=sem, vmem_limit_bytes=VMEM_LIMIT_BYTES)


def _nt(a, b):
    return lax.dot_general(a, b, (((1,), (1,)), ((), ())), preferred_element_type=F32)


def _tn(a, b):
    return lax.dot_general(a, b, (((0,), (0,)), ((), ())), preferred_element_type=F32)


def _dot(a, b):
    return jnp.dot(a, b, preferred_element_type=F32)


def _dot_exact(a, b):
    return jnp.dot(a, b, preferred_element_type=F32, precision=HIGHEST)


def _silu(x):
    return x * jax.nn.sigmoid(x)


def _softplus(x):
    return jnp.maximum(x, 0.0) + jnp.log1p(jnp.exp(-jnp.abs(x)))


def _iota(shape, dim):
    return lax.broadcasted_iota(jnp.int32, shape, dim)


def _rmsnorm_kernel(x_ref, w_ref, o_ref):
    x = x_ref[...]
    ms = jnp.mean(x * x, axis=-1, keepdims=True)
    o_ref[...] = (x * lax.rsqrt(ms + EPS) * w_ref[...]).astype(o_ref.dtype)


def rmsnorm(x, w, out_dtype, tm):
    m, d = x.shape
    return pl.pallas_call(
        _rmsnorm_kernel,
        grid=(m // tm,),
        in_specs=[pl.BlockSpec((tm, d), lambda i: (i, 0)),
                  pl.BlockSpec((1, d), lambda i: (0, 0))],
        out_specs=pl.BlockSpec((tm, d), lambda i: (i, 0)),
        out_shape=jax.ShapeDtypeStruct((m, d), out_dtype),
        compiler_params=_params(("parallel",)),
        name="rmsnorm",
    )(x, w.reshape(1, d).astype(F32))


CAST_BLOCK_BYTES = 8 * 1024 * 1024


def _cast_kernel(x_ref, o_ref):
    o_ref[...] = x_ref[...].astype(o_ref.dtype)


def cast_bf16(w, layer):
    _, r, c = w.shape
    rows = min(r, CAST_BLOCK_BYTES // (4 * c))
    assert r % rows == 0 and rows % 16 == 0
    return pl.pallas_call(
        _cast_kernel,
        grid=(r // rows,),
        in_specs=[pl.BlockSpec((None, rows, c), lambda i: (layer, i, 0))],
        out_specs=pl.BlockSpec((rows, c), lambda i: (i, 0)),
        out_shape=jax.ShapeDtypeStruct((r, c), BF16),
        compiler_params=_params(("parallel",)),
        name="cast_bf16",
    )(w.astype(F32))


def _cast_w_in_kernel(x_ref, main_ref, dt_ref):
    x0, dt0 = D_INNER, D_INNER + SSD_CONV_DIM
    rest0 = dt0 + 2 * SSD_HEADS
    main_ref[:, COL_XBC:COL_Z] = x_ref[:, x0:dt0].astype(BF16)
    main_ref[:, COL_Z:COL_SWA_Q] = x_ref[:, 0:x0].astype(BF16)
    main_ref[:, COL_SWA_Q:] = x_ref[:, rest0:].astype(BF16)
    lane = _iota((x_ref.shape[0], LANE), 1)
    dt_ref[...] = jnp.where(lane < 2 * SSD_HEADS, x_ref[:, dt0:dt0 + LANE], 0.0).astype(BF16)


def cast_w_in(w_in, layer):
    _, r, c = w_in.shape
    rows = 128
    return pl.pallas_call(
        _cast_w_in_kernel,
        grid=(r // rows,),
        in_specs=[pl.BlockSpec((None, rows, c), lambda i: (layer, i, 0))],
        out_specs=[pl.BlockSpec((rows, P_COLS), lambda i: (i, 0)),
                   pl.BlockSpec((rows, LANE), lambda i: (i, 0))],
        out_shape=[jax.ShapeDtypeStruct((r, P_COLS), BF16), jax.ShapeDtypeStruct((r, LANE), BF16)],
        compiler_params=_params(("parallel",)),
        name="cast_w_in",
    )(w_in.astype(F32))


def _mm_kernel(*refs, n_in, epilogue, nk):
    x_refs, w_refs, rest = refs[:n_in], refs[n_in:2 * n_in], refs[2 * n_in:]
    if epilogue == "residual":
        r_ref, o_ref = rest
    else:
        (o_ref,) = rest
    part = _dot(x_refs[0][...], w_refs[0][...])
    for x_ref, w_ref in zip(x_refs[1:], w_refs[1:]):
        part = part + _dot(x_ref[...], w_ref[...])
    if nk == 1:
        if epilogue == "relu2":
            part = jnp.square(jnp.maximum(part, 0.0))
        elif epilogue == "residual":
            part = part + r_ref[...]
        o_ref[...] = part.astype(o_ref.dtype)
    else:
        @pl.when(pl.program_id(2) == 0)
        def _():
            o_ref[...] = r_ref[...]

        o_ref[...] += part


def matmul(xs, w, *, tm, tn, tk, out_dtype, epilogue="none", residual=None, alias=True):
    xs = list(xs) if isinstance(xs, (list, tuple)) else [xs]
    n_in = len(xs)
    m, kc = xs[0].shape
    n = w.shape[1]
    if n_in > 1:
        assert tk == kc and w.shape[0] == n_in * kc
        nk = 1
        w_specs = [pl.BlockSpec((kc, tn), functools.partial(lambda i, j, k, c: (c, j), c=c))
                   for c in range(n_in)]
    else:
        nk = kc // tk
        w_specs = [pl.BlockSpec((tk, tn), lambda i, j, k: (k, j))]
    assert nk == 1 or (epilogue == "residual" and out_dtype == F32)
    in_specs = [pl.BlockSpec((tm, tk), lambda i, j, k: (i, k)) for _ in xs] + w_specs
    args = xs + [w] * n_in
    aliases = {}
    if epilogue == "residual":
        in_specs.append(pl.BlockSpec((tm, tn), lambda i, j, k: (i, j)))
        args.append(residual)
        if alias:
            aliases = {2 * n_in: 0}
    return pl.pallas_call(
        functools.partial(_mm_kernel, n_in=n_in, epilogue=epilogue, nk=nk),
        grid=(m // tm, n // tn, nk),
        in_specs=in_specs,
        out_specs=pl.BlockSpec((tm, tn), lambda i, j, k: (i, j)),
        out_shape=jax.ShapeDtypeStruct((m, n), out_dtype),
        input_output_aliases=aliases,
        compiler_params=_params(("parallel", "parallel", "arbitrary")),
        name="matmul_" + epilogue,
    )(*args)


class Group:
    def __init__(self, mbase, bsz, t):
        self.mbase = mbase
        self.bsz = bsz
        self.t = t
        self.nc = t // CHUNK
        self.nc1 = self.nc + 1
        self.lp = self.nc1 * CHUNK

    def rblk(self, rows):
        per = self.t // rows
        return lambda b: b * per

    def mblk(self):
        first = self.mbase // N_META
        return lambda b: first + b


def _meta_front(m_val, width, dtype):
    return jnp.concatenate([jnp.zeros((META_PAD, width), dtype), m_val], axis=0)


def _ssd_pre_kernel(cur_ref, prev_ref, next_ref, meta_ref, dtr_ref, dtm_ref, cw_ref, cb_ref,
                    bias4_ref, alog4_ref,
                    act_ref, cs_ref, dtv_ref, cst_ref, dtt_ref, tot_ref, ext_ref, *, nc):
    c = pl.program_id(1)
    is_meta = c == 0
    half = (SSD_CONV_W - 1) // 2
    row = _iota((CHUNK, 1), 0)
    valid = jnp.logical_or(c > 0, row >= META_PAD)
    ext_ref[0:N_META, :] = prev_ref[...].astype(F32)
    ext_ref[N_META:N_META + CHUNK, :] = cur_ref[...].astype(F32)
    ext_ref[N_META + CHUNK:, :] = next_ref[...].astype(F32)

    @pl.when(is_meta)
    def _():
        ext_ref[0:N_META + META_PAD, :] = jnp.zeros((N_META + META_PAD, SSD_CONV_DIM), F32)
        ext_ref[N_META + META_PAD:N_META + CHUNK, :] = meta_ref[...].astype(F32)

    @pl.when(c == 1)
    def _():
        ext_ref[0:N_META, :] = meta_ref[...].astype(F32)

    @pl.when(c == nc)
    def _():
        ext_ref[N_META + CHUNK:, :] = jnp.zeros((N_META, SSD_CONV_DIM), F32)

    for c0 in range(0, SSD_CONV_DIM, SSD_CONV_TILE):
        cols = slice(c0, c0 + SSD_CONV_TILE)
        acc = jnp.broadcast_to(cb_ref[:, cols], (CHUNK, SSD_CONV_TILE))
        for j in range(SSD_CONV_W):
            acc = acc + cw_ref[j:j + 1, cols] * ext_ref[pl.ds(N_META - half + j, CHUNK), cols]
        act_ref[:, cols] = jnp.where(valid, _silu(acc), 0.0).astype(act_ref.dtype)

    dtx = jnp.where(is_meta, _meta_front(dtm_ref[...], LANE, F32), dtr_ref[...])
    lane = _iota((CHUNK, LANE), 1)
    keep = jnp.logical_and(valid, lane < SSD_HPG)
    ti = _iota((CHUNK, CHUNK), 0)
    si = _iota((CHUNK, CHUNK), 1)
    tri = (si <= ti).astype(F32)
    widen = (_iota((LANE, SSD_GW), 0) == _iota((LANE, SSD_GW), 1) // SSD_HEAD_DIM).astype(F32)
    for k in range(2 * SSD_GROUPS):
        x = dtx if k == 0 else pltpu.roll(dtx, LANE - SSD_HPG * k, axis=1)
        dt = jnp.where(keep, _softplus(x + bias4_ref[k]), 0.0)
        la = dt * (-jnp.exp(alog4_ref[k]))
        incl = _dot_exact(tri, la)
        cs = incl if k < SSD_GROUPS else incl - la
        cs_ref[k] = cs
        dtv_ref[k] = dt
        cst_ref[k] = jnp.transpose(cs)[0:SSD_HPG, :]
        dtt_ref[k] = jnp.transpose(dt)[0:SSD_HPG, :]
        last = jnp.broadcast_to(incl[CHUNK - 1:CHUNK, :], (8, LANE))
        tot_ref[k] = jnp.concatenate([last, _dot_exact(last, widen)], axis=1)


def ssd_pre(pr, pm, dtr, dtm, conv_w, conv_b, bias4, alog4, grp):
    nc, nc1, bsz = grp.nc, grp.nc1, grp.bsz
    w = SSD_CONV_DIM
    r128, r16, mb = grp.rblk(CHUNK), grp.rblk(N_META), grp.mblk()
    per16 = CHUNK // N_META
    ng = 2 * SSD_GROUPS
    in_specs = [
        pl.BlockSpec((CHUNK, w), lambda b, c: (r128(b) + jnp.maximum(c - 1, 0), 0)),
        pl.BlockSpec((N_META, w), lambda b, c: (r16(b) + jnp.maximum((c - 1) * per16 - 1, 0), 0)),
        pl.BlockSpec((N_META, w), lambda b, c: (r16(b) + jnp.minimum(c * per16, nc * per16 - 1), 0)),
        pl.BlockSpec((N_META, w), lambda b, c: (mb(b), 0)),
        pl.BlockSpec((CHUNK, LANE), lambda b, c: (r128(b) + jnp.maximum(c - 1, 0), 0)),
        pl.BlockSpec((N_META, LANE), lambda b, c: (mb(b), 0)),
        pl.BlockSpec((8, w), lambda b, c: (0, 0)),
        pl.BlockSpec((1, w), lambda b, c: (0, 0)),
        pl.BlockSpec((ng, 1, LANE), lambda b, c: (0, 0, 0)),
        pl.BlockSpec((ng, 1, LANE), lambda b, c: (0, 0, 0)),
    ]
    out_specs = [
        pl.BlockSpec((None, CHUNK, w), lambda b, c: (b, c, 0)),
        pl.BlockSpec((None, ng, CHUNK, LANE), lambda b, c: (b, 0, c, 0)),
        pl.BlockSpec((None, ng, CHUNK, LANE), lambda b, c: (b, 0, c, 0)),
        pl.BlockSpec((None, ng, None, SSD_HPG, LANE), lambda b, c: (b, 0, c, 0, 0)),
        pl.BlockSpec((None, ng, None, SSD_HPG, LANE), lambda b, c: (b, 0, c, 0, 0)),
        pl.BlockSpec((None, ng, None, 8, SSD_TOT_W), lambda b, c: (b, 0, c, 0, 0)),
    ]
    out_shape = [
        jax.ShapeDtypeStruct((bsz, grp.lp, w), BF16),
        jax.ShapeDtypeStruct((bsz, ng, grp.lp, LANE), F32),
        jax.ShapeDtypeStruct((bsz, ng, grp.lp, LANE), F32),
        jax.ShapeDtypeStruct((bsz, ng, nc1, SSD_HPG, LANE), F32),
        jax.ShapeDtypeStruct((bsz, ng, nc1, SSD_HPG, LANE), F32),
        jax.ShapeDtypeStruct((bsz, ng, nc1, 8, SSD_TOT_W), F32),
    ]
    return pl.pallas_call(
        functools.partial(_ssd_pre_kernel, nc=nc),
        grid=(bsz, nc1),
        in_specs=in_specs, out_specs=out_specs, out_shape=out_shape,
        scratch_shapes=[pltpu.VMEM((CHUNK + 2 * N_META, w), F32)],
        compiler_params=_params(("parallel", "arbitrary")),
        name="ssd_pre",
    )(pr, pr, pr, pm, dtr, dtm, conv_w, conv_b, bias4, alog4)


def _ssd_scan_kernel(*refs, rev, final):
    if final:
        (xs_ref, b_ref, c_ref, cs_ref, dtv_ref, cst_ref, dtt_ref, tot_ref,
         yb_ref, zr_ref, zm_ref, dsk_ref, nw_ref, o_ref, om_ref, s_ref) = refs
    else:
        (xs_ref, b_ref, c_ref, cs_ref, dtv_ref, cst_ref, dtt_ref, tot_ref, o_ref, s_ref) = refs
    step = pl.program_id(2)

    @pl.when(step == 0)
    def _():
        s_ref[...] = jnp.zeros_like(s_ref)

    xs = xs_ref[...]
    bm = b_ref[...]
    cm = c_ref[...]
    a = cs_ref[...]
    dt = dtv_ref[...]
    a_t = cst_ref[...]
    dt_t = dtt_ref[...]
    tot = tot_ref[0:1, 0:LANE]
    decx = jnp.exp(tot_ref[0:1, LANE:])

    g = _nt(cm, bm)
    ti = _iota((CHUNK, CHUNK), 0)
    si = _iota((CHUNK, CHUNK), 1)
    mask = (si > ti) if rev else (si <= ti)
    lane = _iota((CHUNK, LANE), 1)
    lo_half = lane < SSD_HEAD_DIM
    zero = jnp.zeros((), xs.dtype)
    if rev:
        e1 = jnp.exp(tot - a)
        wst = dt * jnp.exp(a)
    else:
        e1 = jnp.exp(a)
        wst = dt * jnp.exp(tot - a)
    cm32 = cm.astype(F32)
    state = s_ref[...]
    sb = state.astype(BF16)
    pairs, wtiles = [], []
    for p in range(SSD_HPG // 2):
        xp = xs[:, p * LANE:(p + 1) * LANE]
        sp = sb[:, p * LANE:(p + 1) * LANE]
        lhs, rhs = [], []
        for q in range(2):
            h = 2 * p + q
            acol = a[:, h:h + 1]
            arow = a_t[h:h + 1, :]
            e = (arow - acol) if rev else (acol - arow)
            wm = g * jnp.exp(jnp.where(mask, e, NEG)) * dt_t[h:h + 1, :]
            mine = jnp.logical_not(lo_half) if q else lo_half
            lhs += [wm.astype(BF16), (cm32 * e1[:, h:h + 1]).astype(BF16)]
            rhs += [jnp.where(mine, xp, zero), jnp.where(mine, sp, zero)]
        pairs.append(_dot(jnp.concatenate(lhs, axis=1), jnp.concatenate(rhs, axis=0)))
        wtiles.append(jnp.where(lo_half, wst[:, 2 * p:2 * p + 1], wst[:, 2 * p + 1:2 * p + 2]))
    y = jnp.concatenate(pairs, axis=1)
    wx = jnp.concatenate(wtiles, axis=1)
    xw = (xs.astype(F32) * wx).astype(BF16)
    s_ref[...] = decx * state + _tn(bm, xw)

    if final:
        z = jnp.where(step == 0, _meta_front(zm_ref[...], SSD_GW, BF16), zr_ref[...]).astype(F32)
        yt = y + yb_ref[...] + xs.astype(F32) * dsk_ref[...]
        yt = yt * _silu(z)
        ms = jnp.mean(yt * yt, axis=-1, keepdims=True)
        res = (yt * lax.rsqrt(ms + EPS) * nw_ref[...]).astype(o_ref.dtype)

        @pl.when(step == 0)
        def _():
            om_ref[...] = res[META_PAD:, :]

        @pl.when(step > 0)
        def _():
            o_ref[...] = res
    else:
        o_ref[...] = y


def ssd_scan(act, cs4, dt4, cst4, dtt4, tot4, grp, *, rev, ybwd=None, pr=None, pm=None,
             d_skip=None, norm_w=None):
    final = not rev
    nc1, bsz = grp.nc1, grp.bsz
    d = 1 if rev else 0
    cc = (lambda c: nc1 - 1 - c) if rev else (lambda c: c)
    xs_blk = COL_XBC // SSD_GW
    b_blk = (COL_XBC + D_INNER) // LANE
    c_blk = b_blk + SSD_GROUPS
    in_specs = [
        pl.BlockSpec((None, CHUNK, SSD_GW), lambda b, g, c: (b, cc(c), xs_blk + g)),
        pl.BlockSpec((None, CHUNK, LANE), lambda b, g, c: (b, cc(c), b_blk + g)),
        pl.BlockSpec((None, CHUNK, LANE), lambda b, g, c: (b, cc(c), c_blk + g)),
        pl.BlockSpec((None, None, CHUNK, LANE), lambda b, g, c: (b, d * SSD_GROUPS + g, cc(c), 0)),
        pl.BlockSpec((None, None, CHUNK, LANE), lambda b, g, c: (b, d * SSD_GROUPS + g, cc(c), 0)),
        pl.BlockSpec((None, None, None, SSD_HPG, LANE), lambda b, g, c: (b, d * SSD_GROUPS + g, cc(c), 0, 0)),
        pl.BlockSpec((None, None, None, SSD_HPG, LANE), lambda b, g, c: (b, d * SSD_GROUPS + g, cc(c), 0, 0)),
        pl.BlockSpec((None, None, None, 8, SSD_TOT_W), lambda b, g, c: (b, d * SSD_GROUPS + g, cc(c), 0, 0)),
    ]
    args = [act, act, act, cs4, dt4, cst4, dtt4, tot4]
    if final:
        r128, mb = grp.rblk(CHUNK), grp.mblk()
        z_blk = COL_Z // SSD_GW
        in_specs += [
            pl.BlockSpec((None, CHUNK, SSD_GW), lambda b, g, c: (b, c, g)),
            pl.BlockSpec((CHUNK, SSD_GW), lambda b, g, c: (r128(b) + jnp.maximum(c - 1, 0), z_blk + g)),
            pl.BlockSpec((N_META, SSD_GW), lambda b, g, c: (mb(b), z_blk + g)),
            pl.BlockSpec((1, SSD_GW), lambda b, g, c: (0, g)),
            pl.BlockSpec((1, SSD_GW), lambda b, g, c: (0, g)),
        ]
        args += [ybwd, pr, pm, d_skip, norm_w]
        out_specs = [pl.BlockSpec((CHUNK, SSD_GW), lambda b, g, c: (r128(b) + jnp.maximum(c - 1, 0), g)),
                     pl.BlockSpec((N_META, SSD_GW), lambda b, g, c: (b, g))]
        out_shape = [jax.ShapeDtypeStruct((bsz * grp.t, D_INNER), BF16),
                     jax.ShapeDtypeStruct((bsz * N_META, D_INNER), BF16)]
    else:
        out_specs = pl.BlockSpec((None, CHUNK, SSD_GW), lambda b, g, c: (b, cc(c), g))
        out_shape = jax.ShapeDtypeStruct((bsz, grp.lp, D_INNER), F32)
    return pl.pallas_call(
        functools.partial(_ssd_scan_kernel, rev=rev, final=final),
        grid=(bsz, SSD_GROUPS, nc1),
        in_specs=in_specs,
        out_specs=out_specs,
        out_shape=out_shape,
        scratch_shapes=[pltpu.VMEM((SSD_STATE, SSD_GW), F32)],
        compiler_params=_params(("parallel", "parallel", "arbitrary")),
        name="ssd_scan_" + ("bwd" if rev else "fwd"),
    )(*args)


def _ret_scan_kernel(*refs, rev, final, nc1):
    if final:
        (dl_ref, qr_ref, qm_ref, kr_ref, km_ref, vr_ref, vm_ref, yb_ref, gr_ref, gm_ref,
         o_ref, om_ref, s_ref, dm_ref, e1_ref, ws_ref, dec_ref) = refs
    else:
        (dl_ref, qr_ref, qm_ref, kr_ref, km_ref, vr_ref, vm_ref,
         o_ref, s_ref, dm_ref, e1_ref, ws_ref, dec_ref) = refs
    step = pl.program_id(1)
    chunk = (nc1 - 1 - step) if rev else step
    is_meta = chunk == 0

    @pl.when(step == 0)
    def _():
        s_ref[...] = jnp.zeros_like(s_ref)
        ti = _iota((CHUNK, CHUNK), 0)
        si = _iota((CHUNK, CHUNK), 1)
        tf = ti.astype(F32)
        for h in range(RET_HEADS):
            x = jnp.full((CHUNK, CHUNK), dl_ref[1 if rev else 0, h], F32)
            lg = jnp.minimum(x, 0.0) - jnp.log1p(jnp.exp(-jnp.abs(x)))
            if rev:
                dm_ref[h] = jnp.exp(jnp.where(si > ti, (si - ti).astype(F32) * lg, NEG))
                e1_ref[h] = jnp.exp((CHUNK - tf) * lg)
                ws_ref[h] = jnp.exp(tf * lg)
            else:
                dm_ref[h] = jnp.exp(jnp.where(si <= ti, (ti - si).astype(F32) * lg, NEG))
                e1_ref[h] = jnp.exp((tf + 1.0) * lg)
                ws_ref[h] = jnp.exp((CHUNK - 1.0 - tf) * lg)
            dec_ref[h] = jnp.exp(CHUNK * lg)

    def pick(r_ref, m_ref):
        width = r_ref.shape[-1]
        return jnp.where(is_meta, _meta_front(m_ref[...], width, BF16), r_ref[...])

    q = pick(qr_ref, qm_ref)
    k = pick(kr_ref, km_ref)
    v = pick(vr_ref, vm_ref)
    if final:
        gate = pick(gr_ref, gm_ref)
    lane = _iota((CHUNK, LANE), 1)
    zero = jnp.zeros((), BF16)
    scale = jnp.asarray(RET_K_DIM ** -0.5, BF16)
    heads = range(RET_HEADS)
    pairs = [slice((h // 2) * LANE, (h // 2 + 1) * LANE) for h in heads]
    cols = [slice(h * LANE, (h + 1) * LANE) for h in heads]
    qh = [jnp.where((lane >= RET_K_DIM) if h % 2 else (lane < RET_K_DIM), q[:, pairs[h]], zero) * scale
          for h in heads]
    sc = [_nt(qh[h], k[:, pairs[h]]) for h in heads]
    inter = [_dot(qh[h], s_ref[h].astype(BF16)) for h in heads]
    for h in heads:
        vw = (v[:, cols[h]].astype(F32) * ws_ref[h]).astype(BF16)
        s_ref[h] = dec_ref[h] * s_ref[h] + _tn(k[:, pairs[h]], vw)
    ys = [_dot((sc[h] * dm_ref[h]).astype(BF16), v[:, cols[h]]) + e1_ref[h] * inter[h] for h in heads]
    for h in heads:
        y = ys[h]
        if final:
            yt = y + yb_ref[:, cols[h]]
            mu = jnp.mean(yt, axis=-1, keepdims=True)
            dev = yt - mu
            var = jnp.mean(dev * dev, axis=-1, keepdims=True)
            y = dev * lax.rsqrt(var + 1e-5) * _silu(gate[:, cols[h]].astype(F32))
        o_ref[:, cols[h]] = y.astype(o_ref.dtype)

    if final:
        @pl.when(step == 0)
        def _():
            om_ref[...] = o_ref[META_PAD:, :]


def ret_scan(pr, pm, decay, grp, *, rev, ybwd=None):
    final = not rev
    nc1, bsz = grp.nc1, grp.bsz
    r128, mb = grp.rblk(CHUNK), grp.mblk()
    cc = (lambda c: nc1 - 1 - c) if rev else (lambda c: c)
    qk_w = RET_HEADS * RET_K_DIM

    def real(col0, width):
        blk = col0 // width
        return pl.BlockSpec((CHUNK, width), lambda b, c: (r128(b) + jnp.maximum(cc(c) - 1, 0), blk))

    def meta(col0, width):
        blk = col0 // width
        return pl.BlockSpec((N_META, width), lambda b, c: (mb(b), blk))

    in_specs = [pl.BlockSpec(memory_space=pltpu.SMEM),
                real(COL_RET_Q, qk_w), meta(COL_RET_Q, qk_w),
                real(COL_RET_K, qk_w), meta(COL_RET_K, qk_w),
                real(COL_RET_V, GROUP_WIDTH), meta(COL_RET_V, GROUP_WIDTH)]
    args = [decay, pr, pm, pr, pm, pr, pm]
    if final:
        in_specs += [pl.BlockSpec((None, CHUNK, GROUP_WIDTH), lambda b, c: (b, c, 0)),
                     real(COL_RET_G, GROUP_WIDTH), meta(COL_RET_G, GROUP_WIDTH)]
        args += [ybwd, pr, pm]
        out_specs = [pl.BlockSpec((CHUNK, GROUP_WIDTH), lambda b, c: (r128(b) + jnp.maximum(c - 1, 0), 0)),
                     pl.BlockSpec((N_META, GROUP_WIDTH), lambda b, c: (b, 0))]
        out_shape = [jax.ShapeDtypeStruct((bsz * grp.t, GROUP_WIDTH), BF16),
                     jax.ShapeDtypeStruct((bsz * N_META, GROUP_WIDTH), BF16)]
    else:
        out_specs = pl.BlockSpec((None, CHUNK, GROUP_WIDTH), lambda b, c: (b, cc(c), 0))
        out_shape = jax.ShapeDtypeStruct((bsz, grp.lp, GROUP_WIDTH), F32)
    table = pltpu.VMEM((RET_HEADS, CHUNK, CHUNK), F32)
    return pl.pallas_call(
        functools.partial(_ret_scan_kernel, rev=rev, final=final, nc1=nc1),
        grid=(bsz, nc1),
        in_specs=in_specs,
        out_specs=out_specs,
        out_shape=out_shape,
        scratch_shapes=[pltpu.VMEM((RET_HEADS, LANE, RET_V_DIM), F32), table, table, table, table],
        compiler_params=_params(("parallel", "arbitrary")),
        name="ret_scan_" + ("bwd" if rev else "fwd"),
    )(*args)


def _alibi_slope_rows(rows_per_head, nrows, g):
    hd = _iota((nrows, 1), 0) // rows_per_head
    s = jnp.full((nrows, 1), 2.0 ** -SWA_REP, F32)
    for r in range(SWA_REP - 1):
        s = jnp.where(hd == r, 2.0 ** -(r + 1), s)
    return s * jnp.where(g == 1, 2.0 ** -SWA_REP, 1.0)


def _sink_rows(sink_ref, rows_per_head, nrows, g):
    hd = _iota((nrows, 1), 0) // rows_per_head
    s = jnp.full((nrows, 1), sink_ref[g * SWA_REP + SWA_REP - 1], F32)
    for r in range(SWA_REP - 1):
        s = jnp.where(hd == r, sink_ref[g * SWA_REP + r], s)
    return s


def _softmax_with_sink(s, sink, vall):
    m = jnp.maximum(jnp.max(s, axis=-1, keepdims=True), sink)
    p = jnp.exp(s - m)
    denom = jnp.sum(p, axis=-1, keepdims=True) + jnp.exp(sink - m)
    return _dot(p.astype(BF16), vall) / denom


def _swa_kernel(sink_ref, q0_ref, q1_ref, kp_ref, ko_ref, kn_ref, vp_ref, vo_ref, vn_ref, km_ref, vm_ref,
                o_ref, bias_ref, s_ref, p_ref, l_ref, *, nb):
    q_refs = (q0_ref, q1_ref)
    n = pl.program_id(1)
    blk = SWA_BLOCK
    nq = SWA_REP * blk
    nkeys = 4 * blk
    groups = range(SWA_KV_HEADS)

    @pl.when(n == 0)
    def _():
        t = _iota((nq, nkeys), 0) % blk
        col = _iota((nq, nkeys), 1)
        rel = col - blk - t
        dist = jnp.abs(rel)
        band = jnp.logical_and(col < 3 * blk, dist <= SWA_WINDOW)
        is_meta = jnp.logical_and(col >= 3 * blk, col < 3 * blk + N_META)
        for g in groups:
            slope = _alibi_slope_rows(blk, nq, g)
            bias_ref[g] = jnp.where(band, -slope * dist.astype(F32), jnp.where(is_meta, 0.0, NEG))

    pad = jnp.zeros((META_PAD, SWA_KV_HEADS * SWA_HEAD_DIM), BF16)
    kall = jnp.concatenate([kp_ref[...], ko_ref[...], kn_ref[...], km_ref[...], pad], axis=0)
    vall = jnp.concatenate([vp_ref[...], vo_ref[...], vn_ref[...], vm_ref[...], pad], axis=0)
    for g, q_ref in enumerate(q_refs):
        q = q_ref[...]
        qs = jnp.concatenate([q[:, r * blk:(r + 1) * blk] for r in range(SWA_REP)], axis=0)
        s_ref[g] = _nt(qs, kall[:, g * LANE:(g + 1) * LANE])
    col = _iota((1, nkeys), 1)
    lo = jnp.where(n == 0, blk, 0)
    hi = jnp.where(n == nb - 1, 2 * blk, 3 * blk)
    outside = jnp.logical_or(col < lo, jnp.logical_and(col >= hi, col < 3 * blk))
    scale = SWA_HEAD_DIM ** -0.5
    outs = []
    for g in groups:
        sink = _sink_rows(sink_ref, blk, nq, g)
        for c in range(nq // SWA_SM_ROWS):
            rows = slice(c * SWA_SM_ROWS, (c + 1) * SWA_SM_ROWS)
            s = jnp.where(outside, NEG, s_ref[g, rows, :] * scale + bias_ref[g, rows, :])
            sk = sink[rows]
            m = jnp.maximum(jnp.max(s, axis=-1, keepdims=True), sk)
            p = jnp.exp(s - m)
            p_ref[g, rows, :] = p.astype(BF16)
            denom = jnp.sum(p, axis=-1, keepdims=True) + jnp.exp(sk - m)
            l_ref[g, rows, :] = jnp.broadcast_to(1.0 / denom, (SWA_SM_ROWS, LANE))
        o = _dot(p_ref[g], vall[:, g * LANE:(g + 1) * LANE]) * l_ref[g]
        outs += [o[r * blk:(r + 1) * blk] for r in range(SWA_REP)]
    o_ref[...] = jnp.concatenate(outs, axis=1).astype(o_ref.dtype)


def swa_real(pr, pm, sink, grp):
    bsz, nb = grp.bsz, grp.t // SWA_BLOCK
    r128, mb = grp.rblk(SWA_BLOCK), grp.mblk()
    kvw = SWA_KV_HEADS * SWA_HEAD_DIM
    qw = SWA_REP * SWA_HEAD_DIM
    q_blk, k_blk, v_blk = COL_SWA_Q // qw, COL_SWA_K // kvw, COL_SWA_V // kvw
    nq, nkeys = SWA_REP * SWA_BLOCK, 4 * SWA_BLOCK

    def kv(col_blk, off):
        return pl.BlockSpec((SWA_BLOCK, kvw),
                            lambda b, n: (r128(b) + jnp.clip(n + off, 0, nb - 1), col_blk))

    in_specs = [pl.BlockSpec(memory_space=pltpu.SMEM),
                pl.BlockSpec((SWA_BLOCK, qw), lambda b, n: (r128(b) + n, q_blk)),
                pl.BlockSpec((SWA_BLOCK, qw), lambda b, n: (r128(b) + n, q_blk + 1)),
                kv(k_blk, -1), kv(k_blk, 0), kv(k_blk, 1),
                kv(v_blk, -1), kv(v_blk, 0), kv(v_blk, 1),
                pl.BlockSpec((N_META, kvw), lambda b, n: (mb(b), k_blk)),
                pl.BlockSpec((N_META, kvw), lambda b, n: (mb(b), v_blk))]
    return pl.pallas_call(
        functools.partial(_swa_kernel, nb=nb),
        grid=(bsz, nb),
        in_specs=in_specs,
        out_specs=pl.BlockSpec((SWA_BLOCK, GROUP_WIDTH), lambda b, n: (b * nb + n, 0)),
        out_shape=jax.ShapeDtypeStruct((bsz * grp.t, GROUP_WIDTH), BF16),
        scratch_shapes=[pltpu.VMEM((SWA_KV_HEADS, nq, nkeys), F32),
                        pltpu.VMEM((SWA_KV_HEADS, nq, nkeys), F32),
                        pltpu.VMEM((SWA_KV_HEADS, nq, nkeys), BF16),
                        pltpu.VMEM((SWA_KV_HEADS, nq, LANE), F32)],
        compiler_params=_params(("parallel", "arbitrary")),
        name="swa_real",
    )(sink, pr, pr, pr, pr, pr, pr, pr, pr, pm, pm)


def _swa_meta_kernel(sink_ref, q_ref, k0_ref, v0_ref, km_ref, vm_ref, o_ref):
    g = pl.program_id(1)
    nq = SWA_REP * N_META
    nkeys = 2 * SWA_BLOCK
    q = q_ref[...]
    qs = jnp.concatenate([q[:, r * LANE:(r + 1) * LANE] for r in range(SWA_REP)], axis=0)
    pad = jnp.zeros((META_PAD, SWA_HEAD_DIM), BF16)
    kall = jnp.concatenate([k0_ref[...], km_ref[...], pad], axis=0)
    vall = jnp.concatenate([v0_ref[...], vm_ref[...], pad], axis=0)
    i = _iota((nq, nkeys), 0) % N_META
    col = _iota((nq, nkeys), 1)
    mdist = N_META + col - i
    near = jnp.logical_and(col < SWA_BLOCK, mdist <= SWA_WINDOW)
    is_meta = jnp.logical_and(col >= SWA_BLOCK, col < SWA_BLOCK + N_META)
    slope = _alibi_slope_rows(N_META, nq, g)
    bias = jnp.where(near, -slope * mdist.astype(F32), jnp.where(is_meta, 0.0, NEG))
    s = _nt(qs, kall) * (SWA_HEAD_DIM ** -0.5) + bias
    o = _softmax_with_sink(s, _sink_rows(sink_ref, N_META, nq, g), vall)
    o_ref[...] = jnp.concatenate([o[r * N_META:(r + 1) * N_META] for r in range(SWA_REP)],
                                 axis=1).astype(o_ref.dtype)


def swa_meta(pr, pm, sink, grp):
    bsz = grp.bsz
    r128, mb = grp.rblk(SWA_BLOCK), grp.mblk()
    qw = SWA_REP * SWA_HEAD_DIM
    q_blk, k_blk, v_blk = COL_SWA_Q // qw, COL_SWA_K // LANE, COL_SWA_V // LANE
    in_specs = [pl.BlockSpec(memory_space=pltpu.SMEM),
                pl.BlockSpec((N_META, qw), lambda b, g: (mb(b), q_blk + g)),
                pl.BlockSpec((SWA_BLOCK, LANE), lambda b, g: (r128(b), k_blk + g)),
                pl.BlockSpec((SWA_BLOCK, LANE), lambda b, g: (r128(b), v_blk + g)),
                pl.BlockSpec((N_META, LANE), lambda b, g: (mb(b), k_blk + g)),
                pl.BlockSpec((N_META, LANE), lambda b, g: (mb(b), v_blk + g))]
    return pl.pallas_call(
        _swa_meta_kernel,
        grid=(bsz, SWA_KV_HEADS),
        in_specs=in_specs,
        out_specs=pl.BlockSpec((N_META, qw), lambda b, g: (b, g)),
        out_shape=jax.ShapeDtypeStruct((bsz * N_META, GROUP_WIDTH), BF16),
        compiler_params=_params(("parallel", "parallel")),
        name="swa_meta",
    )(sink, pm, pr, pr, pm, pm)


NA_KROWS = 3 * NA_QROWS
NA_RPB_H = 2 * NA_KH - 1
NA_RPB_W = 2 * NA_KW - 1
NA_VARIANTS = 3
NA_SM_ROWS = 32
NA_HPS = 4


def _na_row_ok(variant, i, j):
    if variant == 0:
        return NA_QROWS <= j < NA_QROWS + NA_KH
    if variant == 1:
        return i <= j < i + NA_KH
    return j < NA_KH


def _na_bias_kernel(rpb_ref, o_ref):
    h = pl.program_id(0)
    base = h * (NA_RPB_H * NA_RPB_W)
    qc = _iota((GRID_W, LANE), 0)
    lane = _iota((GRID_W, LANE), 1)
    lo_half = lane < GRID_W
    kc = jnp.where(lo_half, lane, lane - GRID_W)
    d = kc - qc + (NA_KW - 1)
    col_start = jnp.clip(qc - NA_KW // 2, 0, GRID_W - NA_KW)
    col_ok = jnp.logical_and(kc >= col_start, kc < col_start + NA_KW)
    neg = jnp.full((GRID_W, LANE), NEG, F32)
    for a in range(NA_RPB_H - 1):
        val = jnp.zeros((GRID_W, LANE), F32)
        for b in range(NA_RPB_W):
            s_lo = rpb_ref[base + a * NA_RPB_W + b]
            s_hi = rpb_ref[base + (a + 1) * NA_RPB_W + b]
            val = jnp.where(d == b, jnp.where(lo_half, s_lo, s_hi), val)
        tile = jnp.where(col_ok, val, neg)
        for jp in range(NA_KROWS // 2):
            i = 2 * jp + (NA_QROWS - 1) - a
            if not 0 <= i < NA_QROWS:
                continue
            for v in range(NA_VARIANTS):
                ok_lo, ok_hi = _na_row_ok(v, i, 2 * jp), _na_row_ok(v, i, 2 * jp + 1)
                if ok_lo and ok_hi:
                    blk = tile
                elif ok_lo:
                    blk = jnp.where(lo_half, tile, neg)
                elif ok_hi:
                    blk = jnp.where(lo_half, neg, tile)
                else:
                    blk = neg
                o_ref[v, i * GRID_W:(i + 1) * GRID_W, jp * LANE:(jp + 1) * LANE] = blk
    meta_cols = jnp.where(_iota((NA_QB, LANE), 1) < N_META, 0.0, NEG)
    for v in range(NA_VARIANTS):
        o_ref[v, :, 3 * NA_QB:] = meta_cols


def na_bias(rpb):
    return pl.pallas_call(
        _na_bias_kernel,
        grid=(NA_HEADS,),
        in_specs=[pl.BlockSpec(memory_space=pltpu.SMEM)],
        out_specs=pl.BlockSpec((NA_VARIANTS, None, NA_QB, NA_KEYS), lambda h: (0, h, 0, 0)),
        out_shape=jax.ShapeDtypeStruct((NA_VARIANTS, NA_HEADS, NA_QB, NA_KEYS), F32),
        compiler_params=_params(("parallel",)),
        name="na_bias",
    )(rpb.astype(F32).reshape(-1))


def _na_kernel(q_ref, kp_ref, ko_ref, kn_ref, vp_ref, vo_ref, vn_ref, km_ref, vm_ref,
               bias_ref, o_ref, s_ref, p_ref, l_ref):
    q = q_ref[...] * jnp.asarray(NA_HEAD_DIM ** -0.5, BF16)
    width = NA_HPS * NA_HEAD_DIM
    pad = jnp.zeros((META_PAD, width), BF16)
    kall = jnp.concatenate([kp_ref[...], ko_ref[...], kn_ref[...], km_ref[...], pad], axis=0)
    vall = jnp.concatenate([vp_ref[...], vo_ref[...], vn_ref[...], vm_ref[...], pad], axis=0)
    zero = jnp.zeros((), BF16)
    qlane = _iota((NA_QB, LANE), 1)
    vlane = _iota((NA_KEYS, LANE), 1)
    tiles = [slice((hh // 2) * LANE, (hh // 2 + 1) * LANE) for hh in range(NA_HPS)]
    for hh in range(NA_HPS):
        qsel = (qlane >= NA_HEAD_DIM) if hh % 2 else (qlane < NA_HEAD_DIM)
        s_ref[hh] = _nt(jnp.where(qsel, q[:, tiles[hh]], zero), kall[:, tiles[hh]])
    outs = [None] * (NA_HPS // 2)
    for hh in range(NA_HPS):
        for c in range(NA_QB // NA_SM_ROWS):
            rows = slice(c * NA_SM_ROWS, (c + 1) * NA_SM_ROWS)
            s = s_ref[hh, rows, :] + bias_ref[hh, rows, :]
            m = jnp.max(s, axis=-1, keepdims=True)
            p = jnp.exp(s - m)
            p_ref[hh, rows, :] = p.astype(BF16)
            l_ref[hh, rows, :] = jnp.broadcast_to(1.0 / jnp.sum(p, axis=-1, keepdims=True),
                                                  (NA_SM_ROWS, LANE))
        vsel = (vlane >= NA_HEAD_DIM) if hh % 2 else (vlane < NA_HEAD_DIM)
        o = _dot(p_ref[hh], jnp.where(vsel, vall[:, tiles[hh]], zero)) * l_ref[hh]
        outs[hh // 2] = o if outs[hh // 2] is None else outs[hh // 2] + o
    o_ref[...] = jnp.concatenate(outs, axis=1).astype(o_ref.dtype)


def na_real(pr, pm, bias, grp):
    bsz, nqb = grp.bsz, grp.t // NA_QB
    rq, mb = grp.rblk(NA_QB), grp.mblk()
    width = NA_HPS * NA_HEAD_DIM
    q_blk, k_blk, v_blk = COL_NA_Q // width, COL_NA_K // width, COL_NA_V // width

    def kv(col_blk, off):
        return pl.BlockSpec((NA_QB, width),
                            lambda p, b, n: (rq(b) + jnp.clip(n + off, 0, nqb - 1), col_blk + p))

    in_specs = [pl.BlockSpec((NA_QB, width), lambda p, b, n: (rq(b) + n, q_blk + p)),
                kv(k_blk, -1), kv(k_blk, 0), kv(k_blk, 1),
                kv(v_blk, -1), kv(v_blk, 0), kv(v_blk, 1),
                pl.BlockSpec((N_META, width), lambda p, b, n: (mb(b), k_blk + p)),
                pl.BlockSpec((N_META, width), lambda p, b, n: (mb(b), v_blk + p)),
                pl.BlockSpec((None, NA_HPS, NA_QB, NA_KEYS),
                             lambda p, b, n: (jnp.where(n == 0, 0, jnp.where(n == nqb - 1, 2, 1)), p, 0, 0))]
    return pl.pallas_call(
        _na_kernel,
        grid=(NA_HEADS // NA_HPS, bsz, nqb),
        in_specs=in_specs,
        out_specs=pl.BlockSpec((NA_QB, width), lambda p, b, n: (b * nqb + n, p)),
        out_shape=jax.ShapeDtypeStruct((bsz * grp.t, GROUP_WIDTH), BF16),
        scratch_shapes=[pltpu.VMEM((NA_HPS, NA_QB, NA_KEYS), F32),
                        pltpu.VMEM((NA_HPS, NA_QB, NA_KEYS), BF16),
                        pltpu.VMEM((NA_HPS, NA_QB, LANE), F32)],
        compiler_params=_params(("parallel", "parallel", "arbitrary")),
        name="na_real",
    )(pr, pr, pr, pr, pr, pr, pr, pm, pm, bias)


def _na_meta_kernel(q_ref, k_ref, v_ref, km_ref, vm_ref, o_ref):
    nwin = NA_KH * GRID_W
    nkeys = nwin + CHUNK
    q = q_ref[...]
    pad = jnp.zeros((META_PAD, LANE), BF16)
    kall = jnp.concatenate([k_ref[...], km_ref[...], pad], axis=0)
    vall = jnp.concatenate([v_ref[...], vm_ref[...], pad], axis=0)
    col = _iota((N_META, nkeys), 1)
    in_win = jnp.logical_and(col < nwin, col % GRID_W < NA_KW)
    is_meta = jnp.logical_and(col >= nwin, col < nwin + N_META)
    bias = jnp.where(jnp.logical_or(in_win, is_meta), 0.0, NEG)
    zero = jnp.zeros((), BF16)
    qlane = _iota((N_META, LANE), 1)
    vlane = _iota((nkeys, LANE), 1)
    out = jnp.zeros((N_META, LANE), F32)
    for hh in range(2):
        qsel = (qlane >= NA_HEAD_DIM) if hh else (qlane < NA_HEAD_DIM)
        vsel = (vlane >= NA_HEAD_DIM) if hh else (vlane < NA_HEAD_DIM)
        s = _nt(jnp.where(qsel, q, zero), kall) * (NA_HEAD_DIM ** -0.5) + bias
        m = jnp.max(s, axis=-1, keepdims=True)
        p = jnp.exp(s - m)
        denom = jnp.sum(p, axis=-1, keepdims=True)
        out = out + _dot(p.astype(BF16), jnp.where(vsel, vall, zero)) / denom
    o_ref[...] = out.astype(o_ref.dtype)


def na_meta(pr, pm, grp):
    bsz = grp.bsz
    nwin = NA_KH * GRID_W
    rw, mb = grp.rblk(nwin), grp.mblk()
    q_blk, k_blk, v_blk = COL_NA_Q // LANE, COL_NA_K // LANE, COL_NA_V // LANE
    in_specs = [pl.BlockSpec((N_META, LANE), lambda b, p: (mb(b), q_blk + p)),
                pl.BlockSpec((nwin, LANE), lambda b, p: (rw(b), k_blk + p)),
                pl.BlockSpec((nwin, LANE), lambda b, p: (rw(b), v_blk + p)),
                pl.BlockSpec((N_META, LANE), lambda b, p: (mb(b), k_blk + p)),
                pl.BlockSpec((N_META, LANE), lambda b, p: (mb(b), v_blk + p))]
    return pl.pallas_call(
        _na_meta_kernel,
        grid=(bsz, NA_HEADS // 2),
        in_specs=in_specs,
        out_specs=pl.BlockSpec((N_META, LANE), lambda b, p: (b, p)),
        out_shape=jax.ShapeDtypeStruct((bsz * N_META, GROUP_WIDTH), BF16),
        compiler_params=_params(("parallel", "parallel")),
        name="na_meta",
    )(pm, pr, pr, pm, pm)


def mix_group(pr, pm, dtr, dtm, lw, grp, need_meta):
    act, cs4, dt4, cst4, dtt4, tot4 = ssd_pre(pr, pm, dtr, dtm, lw["conv_w"], lw["conv_b"],
                                               lw["bias4"], lw["alog4"], grp)
    yb = ssd_scan(act, cs4, dt4, cst4, dtt4, tot4, grp, rev=True)
    y_ssd, ym_ssd = ssd_scan(act, cs4, dt4, cst4, dtt4, tot4, grp, rev=False, ybwd=yb, pr=pr, pm=pm,
                             d_skip=lw["d_skip"], norm_w=lw["ssd_norm_w"])
    rb = ret_scan(pr, pm, lw["ret_decay"], grp, rev=True)
    y_ret, ym_ret = ret_scan(pr, pm, lw["ret_decay"], grp, rev=False, ybwd=rb)
    real = [y_ssd, swa_real(pr, pm, lw["sink"], grp), na_real(pr, pm, lw["na_bias"], grp), y_ret]
    meta = None
    if need_meta:
        meta = [ym_ssd, swa_meta(pr, pm, lw["sink"], grp), na_meta(pr, pm, grp), ym_ret]
    return real, meta


def _layer_weights(i, w_in, ssd_conv_w, ssd_conv_b, ssd_dt_bias, ssd_a_log, ssd_d, ssd_norm_w,
                   swa_sink, na_rpb, ret_decay, w_out, w_up, w_down):
    w_main, w_dt = cast_w_in(w_in, i)

    def per_dir_group(v):
        v4 = v.astype(F32).reshape(2 * SSD_GROUPS, 1, SSD_HPG)
        return jnp.pad(v4, ((0, 0), (0, 0), (0, LANE - SSD_HPG)))

    return dict(
        w_main=w_main, w_dt=w_dt,
        conv_w=jnp.pad(ssd_conv_w[i].astype(F32), ((0, 8 - SSD_CONV_W), (0, 0))),
        conv_b=ssd_conv_b[i].astype(F32).reshape(1, SSD_CONV_DIM),
        bias4=per_dir_group(ssd_dt_bias[i]), alog4=per_dir_group(ssd_a_log[i]),
        d_skip=jnp.repeat(ssd_d[i].astype(F32), SSD_HEAD_DIM).reshape(1, D_INNER),
        ssd_norm_w=ssd_norm_w[i].astype(F32).reshape(1, D_INNER),
        sink=swa_sink[i].astype(F32), na_bias=na_bias(na_rpb[i]),
        ret_decay=ret_decay[i].astype(F32),
        w_out=cast_bf16(w_out, i), w_up=cast_bf16(w_up, i), w_down=cast_bf16(w_down, i),
    )


TM_REAL = 1024
TN = 512
TN_BF16 = 1024
TK_DOWN = 4096
TM_NORM = 256


def _mlp(x, tm, lw, norm_w):
    n2 = rmsnorm(x, norm_w, BF16, min(tm, TM_NORM))
    u = matmul(n2, lw["w_up"], tm=tm, tn=TN_BF16, tk=D_MODEL, out_dtype=BF16, epilogue="relu2")
    return matmul(u, lw["w_down"], tm=tm, tn=TN, tk=TK_DOWN, out_dtype=F32,
                  epilogue="residual", residual=x)


def kernel(x_prompt, x_sample, meta_tokens, norm1_w, w_in, ssd_conv_w, ssd_conv_b, ssd_dt_bias,
           ssd_a_log, ssd_d, ssd_norm_w, swa_sink, na_rpb, ret_decay, w_out, norm2_w, w_up, w_down,
           final_norm_w):
    d = D_MODEL
    inputs = [x_prompt, x_sample]
    groups, mbase = [], 0
    for x in inputs:
        groups.append(Group(mbase, x.shape[0], x.shape[1]))
        mbase += x.shape[0] * N_META
    n_meta = mbase
    xs = [x.reshape(-1, d).astype(F32) for x in inputs]
    xm = jnp.tile(meta_tokens.astype(F32), (n_meta // N_META, 1))

    for i in range(DEPTH):
        last = i == DEPTH - 1
        lw = _layer_weights(i, w_in, ssd_conv_w, ssd_conv_b, ssd_dt_bias, ssd_a_log, ssd_d,
                            ssd_norm_w, swa_sink, na_rpb, ret_decay, w_out, w_up, w_down)
        nm = rmsnorm(xm, norm1_w[i], BF16, n_meta)
        pm = matmul(nm, lw["w_main"], tm=n_meta, tn=TN_BF16, tk=d, out_dtype=BF16)
        dtm = matmul(nm, lw["w_dt"], tm=n_meta, tn=LANE, tk=d, out_dtype=F32)
        meta_parts = []
        for gi, grp in enumerate(groups):
            nr = rmsnorm(xs[gi], norm1_w[i], BF16, TM_NORM)
            pr = matmul(nr, lw["w_main"], tm=TM_REAL, tn=TN_BF16, tk=d, out_dtype=BF16)
            dtr = matmul(nr, lw["w_dt"], tm=TM_REAL, tn=LANE, tk=d, out_dtype=F32)
            real, meta = mix_group(pr, pm, dtr, dtm, lw, grp, need_meta=not last)
            meta_parts.append(meta)
            h = matmul(real, lw["w_out"], tm=TM_REAL, tn=TN, tk=GROUP_WIDTH, out_dtype=F32,
                       epilogue="residual", residual=xs[gi], alias=i > 0)
            xs[gi] = _mlp(h, TM_REAL, lw, norm2_w[i])
        if not last:
            mixed_m = [jnp.concatenate([m[k] for m in meta_parts], axis=0) for k in range(4)]
            hm = matmul(mixed_m, lw["w_out"], tm=n_meta, tn=TN, tk=GROUP_WIDTH, out_dtype=F32,
                        epilogue="residual", residual=xm)
            xm = _mlp(hm, n_meta, lw, norm2_w[i])

    outs = [rmsnorm(x, final_norm_w, F32, TM_NORM).reshape(inp.shape)
            for x, inp in zip(xs, inputs)]
    return tuple(outs)
```

```python
import functools
import numpy as np
import jax
import jax.numpy as jnp
from jax import lax
from jax.experimental import pallas as pl
from jax.experimental.pallas import tpu as pltpu

F32 = jnp.float32
BF16 = jnp.bfloat16
HIGHEST = lax.Precision.HIGHEST

D_MODEL = 4096
DEPTH = 2
N_META = 16
GRID_W = 64
GROUP_WIDTH = D_MODEL // 4
D_FF = 4 * D_MODEL
EPS = 1e-6
CHUNK = 128
META_PAD = CHUNK - N_META

SSD_HEAD_DIM = 64
SSD_HEADS = 16
SSD_GROUPS = 2
SSD_HPG = 8
SSD_STATE = 128
SSD_CONV_W = 5
D_INNER = GROUP_WIDTH
SSD_CONV_DIM = D_INNER + 2 * SSD_GROUPS * SSD_STATE
SSD_GW = SSD_HPG * SSD_HEAD_DIM
SSD_CONV_TILE = 256
SSD_TOT_W = 128 + SSD_GW + SSD_HPG * 128

SWA_HEAD_DIM = 128
SWA_HEADS = 8
SWA_KV_HEADS = 2
SWA_REP = SWA_HEADS // SWA_KV_HEADS
SWA_WINDOW = 128
SWA_BLOCK = 128
SWA_SM_ROWS = 32

NA_HEAD_DIM = 64
NA_HEADS = 16
NA_KH = 8
NA_KW = 16
NA_QROWS = 4
NA_QB = NA_QROWS * GRID_W
NA_KEYS = 3 * NA_QB + CHUNK

RET_HEADS = 8
RET_K_DIM = 64
RET_V_DIM = 128

LANE = 128
COL_XBC = 0
COL_Z = 1536
COL_SWA_Q = 2560
COL_SWA_K = 3584
COL_SWA_V = 3840
COL_NA_Q = 4096
COL_NA_K = 5120
COL_NA_V = 6144
COL_RET_Q = 7168
COL_RET_K = 7680
COL_RET_V = 8192
COL_RET_G = 9216
P_COLS = 10240

NEG = -1e30
VMEM_LIMIT_BYTES = 56 * 1024 * 1024


def _params(sem):
    return pltpu.CompilerParams(dimension_semantics=sem, vmem_limit_bytes=VMEM_LIMIT_BYTES)


def _nt(a, b):
    return lax.dot_general(a, b, (((1,), (1,)), ((), ())), preferred_element_type=F32)


def _tn(a, b):
    return lax.dot_general(a, b, (((0,), (0,)), ((), ())), preferred_element_type=F32)


def _dot(a, b):
    return jnp.dot(a, b, preferred_element_type=F32)


def _dot_exact(a, b):
    return jnp.dot(a, b, preferred_element_type=F32, precision=HIGHEST)


def _silu(x):
    return x * jax.nn.sigmoid(x)


def _softplus(x):
    return jnp.maximum(x, 0.0) + jnp.log1p(jnp.exp(-jnp.abs(x)))


def _iota(shape, dim):
    return lax.broadcasted_iota(jnp.int32, shape, dim)


def _rmsnorm_kernel(x_ref, w_ref, o_ref):
    x = x_ref[...]
    ms = jnp.mean(x * x, axis=-1, keepdims=True)
    o_ref[...] = (x * lax.rsqrt(ms + EPS) * w_ref[...]).astype(o_ref.dtype)


def rmsnorm(x, w, out_dtype, tm):
    m, d = x.shape
    return pl.pallas_call(
        _rmsnorm_kernel,
        grid=(m // tm,),
        in_specs=[pl.BlockSpec((tm, d), lambda i: (i, 0)),
                  pl.BlockSpec((1, d), lambda i: (0, 0))],
        out_specs=pl.BlockSpec((tm, d), lambda i: (i, 0)),
        out_shape=jax.ShapeDtypeStruct((m, d), out_dtype),
        compiler_params=_params(("parallel",)),
        name="rmsnorm",
    )(x, w.reshape(1, d).astype(F32))


CAST_BLOCK_BYTES = 8 * 1024 * 1024


def _cast_kernel(x_ref, o_ref):
    o_ref[...] = x_ref[...].astype(o_ref.dtype)


def cast_bf16(w, layer):
    _, r, c = w.shape
    rows = min(r, CAST_BLOCK_BYTES // (4 * c))
    assert r % rows == 0 and rows % 16 == 0
    return pl.pallas_call(
        _cast_kernel,
        grid=(r // rows,),
        in_specs=[pl.BlockSpec((None, rows, c), lambda i: (layer, i, 0))],
        out_specs=pl.BlockSpec((rows, c), lambda i: (i, 0)),
        out_shape=jax.ShapeDtypeStruct((r, c), BF16),
        compiler_params=_params(("parallel",)),
        name="cast_bf16",
    )(w.astype(F32))


def _cast_w_in_kernel(x_ref, main_ref, dt_ref):
    x0, dt0 = D_INNER, D_INNER + SSD_CONV_DIM
    rest0 = dt0 + 2 * SSD_HEADS
    main_ref[:, COL_XBC:COL_Z] = x_ref[:, x0:dt0].astype(BF16)
    main_ref[:, COL_Z:COL_SWA_Q] = x_ref[:, 0:x0].astype(BF16)
    main_ref[:, COL_SWA_Q:] = x_ref[:, rest0:].astype(BF16)
    lane = _iota((x_ref.shape[0], LANE), 1)
    dt_ref[...] = jnp.where(lane < 2 * SSD_HEADS, x_ref[:, dt0:dt0 + LANE], 0.0).astype(BF16)


def cast_w_in(w_in, layer):
    _, r, c = w_in.shape
    rows = 128
    return pl.pallas_call(
        _cast_w_in_kernel,
        grid=(r // rows,),
        in_specs=[pl.BlockSpec((None, rows, c), lambda i: (layer, i, 0))],
        out_specs=[pl.BlockSpec((rows, P_COLS), lambda i: (i, 0)),
                   pl.BlockSpec((rows, LANE), lambda i: (i, 0))],
        out_shape=[jax.ShapeDtypeStruct((r, P_COLS), BF16), jax.ShapeDtypeStruct((r, LANE), BF16)],
        compiler_params=_params(("parallel",)),
        name="cast_w_in",
    )(w_in.astype(F32))


def _mm_kernel(*refs, n_in, epilogue, nk):
    x_refs, w_refs, rest = refs[:n_in], refs[n_in:2 * n_in], refs[2 * n_in:]
    if epilogue == "residual":
        r_ref, o_ref = rest
    else:
        (o_ref,) = rest
    part = _dot(x_refs[0][...], w_refs[0][...])
    for x_ref, w_ref in zip(x_refs[1:], w_refs[1:]):
        part = part + _dot(x_ref[...], w_ref[...])
    if nk == 1:
        if epilogue == "relu2":
            part = jnp.square(jnp.maximum(part, 0.0))
        elif epilogue == "residual":
            part = part + r_ref[...]
        o_ref[...] = part.astype(o_ref.dtype)
    else:
        @pl.when(pl.program_id(2) == 0)
        def _():
            o_ref[...] = r_ref[...]

        o_ref[...] += part


def matmul(xs, w, *, tm, tn, tk, out_dtype, epilogue="none", residual=None, alias=True):
    xs = list(xs) if isinstance(xs, (list, tuple)) else [xs]
    n_in = len(xs)
    m, kc = xs[0].shape
    n = w.shape[1]
    if n_in > 1:
        assert tk == kc and w.shape[0] == n_in * kc
        nk = 1
        w_specs = [pl.BlockSpec((kc, tn), functools.partial(lambda i, j, k, c: (c, j), c=c))
                   for c in range(n_in)]
    else:
        nk = kc // tk
        w_specs = [pl.BlockSpec((tk, tn), lambda i, j, k: (k, j))]
    assert nk == 1 or (epilogue == "residual" and out_dtype == F32)
    in_specs = [pl.BlockSpec((tm, tk), lambda i, j, k: (i, k)) for _ in xs] + w_specs
    args = xs + [w] * n_in
    aliases = {}
    if epilogue == "residual":
        in_specs.append(pl.BlockSpec((tm, tn), lambda i, j, k: (i, j)))
        args.append(residual)
        if alias:
            aliases = {2 * n_in: 0}
    return pl.pallas_call(
        functools.partial(_mm_kernel, n_in=n_in, epilogue=epilogue, nk=nk),
        grid=(m // tm, n // tn, nk),
        in_specs=in_specs,
        out_specs=pl.BlockSpec((tm, tn), lambda i, j, k: (i, j)),
        out_shape=jax.ShapeDtypeStruct((m, n), out_dtype),
        input_output_aliases=aliases,
        compiler_params=_params(("parallel", "parallel", "arbitrary")),
        name="matmul_" + epilogue,
    )(*args)


class Group:
    def __init__(self, mbase, bsz, t):
        self.mbase = mbase
        self.bsz = bsz
        self.t = t
        self.nc = t // CHUNK
        self.nc1 = self.nc + 1
        self.lp = self.nc1 * CHUNK

    def rblk(self, rows):
        per = self.t // rows
        return lambda b: b * per

    def mblk(self):
        first = self.mbase // N_META
        return lambda b: first + b


def _meta_front(m_val, width, dtype):
    return jnp.concatenate([jnp.zeros((META_PAD, width), dtype), m_val], axis=0)


def _ssd_pre_kernel(cur_ref, prev_ref, next_ref, meta_ref, dtr_ref, dtm_ref, cw_ref, cb_ref,
                    bias4_ref, alog4_ref,
                    act_ref, cs_ref, dtv_ref, cst_ref, dtt_ref, tot_ref, ext_ref, *, nc):
    c = pl.program_id(1)
    is_meta = c == 0
    half = (SSD_CONV_W - 1) // 2
    row = _iota((CHUNK, 1), 0)
    valid = jnp.logical_or(c > 0, row >= META_PAD)
    ext_ref[0:N_META, :] = prev_ref[...].astype(F32)
    ext_ref[N_META:N_META + CHUNK, :] = cur_ref[...].astype(F32)
    ext_ref[N_META + CHUNK:, :] = next_ref[...].astype(F32)

    @pl.when(is_meta)
    def _():
        ext_ref[0:N_META + META_PAD, :] = jnp.zeros((N_META + META_PAD, SSD_CONV_DIM), F32)
        ext_ref[N_META + META_PAD:N_META + CHUNK, :] = meta_ref[...].astype(F32)

    @pl.when(c == 1)
    def _():
        ext_ref[0:N_META, :] = meta_ref[...].astype(F32)

    @pl.when(c == nc)
    def _():
        ext_ref[N_META + CHUNK:, :] = jnp.zeros((N_META, SSD_CONV_DIM), F32)

    for c0 in range(0, SSD_CONV_DIM, SSD_CONV_TILE):
        cols = slice(c0, c0 + SSD_CONV_TILE)
        acc = jnp.broadcast_to(cb_ref[:, cols], (CHUNK, SSD_CONV_TILE))
        for j in range(SSD_CONV_W):
            acc = acc + cw_ref[j:j + 1, cols] * ext_ref[pl.ds(N_META - half + j, CHUNK), cols]
        act_ref[:, cols] = jnp.where(valid, _silu(acc), 0.0).astype(act_ref.dtype)

    dtx = jnp.where(is_meta, _meta_front(dtm_ref[...], LANE, F32), dtr_ref[...])
    lane = _iota((CHUNK, LANE), 1)
    keep = jnp.logical_and(valid, lane < SSD_HPG)
    ti = _iota((CHUNK, CHUNK), 0)
    si = _iota((CHUNK, CHUNK), 1)
    tri = (si <= ti).astype(F32)
    for k in range(2 * SSD_GROUPS):
        x = dtx if k == 0 else pltpu.roll(dtx, LANE - SSD_HPG * k, axis=1)
        dt = jnp.where(keep, _softplus(x + bias4_ref[k]), 0.0)
        la = dt * (-jnp.exp(alog4_ref[k]))
        incl = _dot_exact(tri, la)
        cs = incl if k < SSD_GROUPS else incl - la
        cs_ref[k] = cs
        dtv_ref[k] = dt
        cst_ref[k] = jnp.transpose(cs)[0:SSD_HPG, :]
        dtt_ref[k] = jnp.transpose(dt)[0:SSD_HPG, :]
        last = jnp.broadcast_to(incl[CHUNK - 1:CHUNK, :], (8, LANE))
        tiles = [jnp.broadcast_to(last[:, h:h + 1], (8, LANE)) for h in range(SSD_HPG)]
        halves = [jnp.where(lane[0:8] < SSD_HEAD_DIM, tiles[2 * p], tiles[2 * p + 1])
                  for p in range(SSD_HPG // 2)]
        tot_ref[k] = jnp.concatenate([last] + halves + tiles, axis=1)


def ssd_pre(pr, pm, dtr, dtm, conv_w, conv_b, bias4, alog4, grp):
    nc, nc1, bsz = grp.nc, grp.nc1, grp.bsz
    w = SSD_CONV_DIM
    r128, r16, mb = grp.rblk(CHUNK), grp.rblk(N_META), grp.mblk()
    per16 = CHUNK // N_META
    ng = 2 * SSD_GROUPS
    in_specs = [
        pl.BlockSpec((CHUNK, w), lambda b, c: (r128(b) + jnp.maximum(c - 1, 0), 0)),
        pl.BlockSpec((N_META, w), lambda b, c: (r16(b) + jnp.maximum((c - 1) * per16 - 1, 0), 0)),
        pl.BlockSpec((N_META, w), lambda b, c: (r16(b) + jnp.minimum(c * per16, nc * per16 - 1), 0)),
        pl.BlockSpec((N_META, w), lambda b, c: (mb(b), 0)),
        pl.BlockSpec((CHUNK, LANE), lambda b, c: (r128(b) + jnp.maximum(c - 1, 0), 0)),
        pl.BlockSpec((N_META, LANE), lambda b, c: (mb(b), 0)),
        pl.BlockSpec((8, w), lambda b, c: (0, 0)),
        pl.BlockSpec((1, w), lambda b, c: (0, 0)),
        pl.BlockSpec((ng, 1, LANE), lambda b, c: (0, 0, 0)),
        pl.BlockSpec((ng, 1, LANE), lambda b, c: (0, 0, 0)),
    ]
    out_specs = [
        pl.BlockSpec((None, CHUNK, w), lambda b, c: (b, c, 0)),
        pl.BlockSpec((None, ng, CHUNK, LANE), lambda b, c: (b, 0, c, 0)),
        pl.BlockSpec((None, ng, CHUNK, LANE), lambda b, c: (b, 0, c, 0)),
        pl.BlockSpec((None, ng, None, SSD_HPG, LANE), lambda b, c: (b, 0, c, 0, 0)),
        pl.BlockSpec((None, ng, None, SSD_HPG, LANE), lambda b, c: (b, 0, c, 0, 0)),
        pl.BlockSpec((None, ng, None, 8, SSD_TOT_W), lambda b, c: (b, 0, c, 0, 0)),
    ]
    out_shape = [
        jax.ShapeDtypeStruct((bsz, grp.lp, w), BF16),
        jax.ShapeDtypeStruct((bsz, ng, grp.lp, LANE), F32),
        jax.ShapeDtypeStruct((bsz, ng, grp.lp, LANE), F32),
        jax.ShapeDtypeStruct((bsz, ng, nc1, SSD_HPG, LANE), F32),
        jax.ShapeDtypeStruct((bsz, ng, nc1, SSD_HPG, LANE), F32),
        jax.ShapeDtypeStruct((bsz, ng, nc1, 8, SSD_TOT_W), F32),
    ]
    return pl.pallas_call(
        functools.partial(_ssd_pre_kernel, nc=nc),
        grid=(bsz, nc1),
        in_specs=in_specs, out_specs=out_specs, out_shape=out_shape,
        scratch_shapes=[pltpu.VMEM((CHUNK + 2 * N_META, w), F32)],
        compiler_params=_params(("parallel", "arbitrary")),
        name="ssd_pre",
    )(pr, pr, pr, pm, dtr, dtm, conv_w, conv_b, bias4, alog4)


def _widen_heads(x):
    sel = (_iota((LANE, SSD_GW), 0) == _iota((LANE, SSD_GW), 1) // SSD_HEAD_DIM).astype(BF16)
    hi = x.astype(BF16)
    r1 = x - hi.astype(F32)
    mid = r1.astype(BF16)
    lo = (r1 - mid.astype(F32)).astype(BF16)
    return _dot(jnp.concatenate([hi, mid, lo], axis=1), jnp.concatenate([sel, sel, sel], axis=0))


def _ssd_scan_kernel(*refs, rev, final):
    if final:
        (xs_ref, b_ref, c_ref, cs_ref, dtv_ref, cst_ref, dtt_ref, tot_ref,
         yb_ref, zr0_ref, zr1_ref, zm0_ref, zm1_ref, dsk_ref, nw_ref, o_ref, om_ref, s_ref) = refs
        z_refs = ((zr0_ref, zm0_ref), (zr1_ref, zm1_ref))
    else:
        (xs_ref, b_ref, c_ref, cs_ref, dtv_ref, cst_ref, dtt_ref, tot_ref, o_ref, s_ref) = refs
    step = pl.program_id(1)

    @pl.when(step == 0)
    def _():
        s_ref[...] = jnp.zeros_like(s_ref)

    ti = _iota((CHUNK, CHUNK), 0)
    si = _iota((CHUNK, CHUNK), 1)
    mask = (si > ti) if rev else (si <= ti)
    lane = _iota((CHUNK, LANE), 1)
    lo_half = lane < SSD_HEAD_DIM
    zero = jnp.zeros((), BF16)

    for grp in range(SSD_GROUPS):
        cols = slice(grp * SSD_GW, (grp + 1) * SSD_GW)
        xs = xs_ref[:, cols]
        bm = b_ref[:, grp * LANE:(grp + 1) * LANE]
        cm = c_ref[:, grp * LANE:(grp + 1) * LANE]
        a = cs_ref[grp]
        dt = dtv_ref[grp]
        a_t = cst_ref[grp]
        dt_t = dtt_ref[grp]
        tot = tot_ref[grp, 0:1, 0:LANE]
        decx = jnp.exp(tot_ref[grp, 0:1, LANE:LANE + SSD_GW])

        g = _nt(cm, bm)
        if rev:
            wst = dt * jnp.exp(a)
        else:
            wst = dt * jnp.exp(tot - a)
        cm32 = cm.astype(F32)
        state = s_ref[grp]
        sb = state.astype(BF16)
        pairs = []
        for p in range(SSD_HPG // 2):
            xp = xs[:, p * LANE:(p + 1) * LANE]
            sp = sb[:, p * LANE:(p + 1) * LANE]
            lhs, rhs = [], []
            for q in range(2):
                h = 2 * p + q
                acol = jnp.broadcast_to(a[:, h:h + 1], (CHUNK, CHUNK))
                arow = a_t[h:h + 1, :]
                e = (arow - acol) if rev else (acol - arow)
                wm = g * jnp.exp(jnp.where(mask, e, NEG)) * dt_t[h:h + 1, :]
                if rev:
                    t0 = LANE + SSD_GW + h * LANE
                    e1 = jnp.exp(tot_ref[grp, 0:1, t0:t0 + LANE] - acol)
                else:
                    e1 = jnp.exp(acol)
                mine = jnp.logical_not(lo_half) if q else lo_half
                lhs += [wm.astype(BF16), (cm32 * e1).astype(BF16)]
                rhs += [jnp.where(mine, xp, zero), jnp.where(mine, sp, zero)]
            pairs.append(_dot(jnp.concatenate(lhs, axis=1), jnp.concatenate(rhs, axis=0)))
        y = jnp.concatenate(pairs, axis=1)
        xw = (xs.astype(F32) * _widen_heads(wst)).astype(BF16)
        s_ref[grp] = decx * state + _tn(bm, xw)

        if final:
            zr_ref, zm_ref = z_refs[grp]
            z = jnp.where(step == 0, _meta_front(zm_ref[...], SSD_GW, BF16), zr_ref[...]).astype(F32)
            yt = y + yb_ref[:, cols] + xs.astype(F32) * dsk_ref[:, cols]
            yt = yt * _silu(z)
            ms = jnp.mean(yt * yt, axis=-1, keepdims=True)
            y = yt * lax.rsqrt(ms + EPS) * nw_ref[:, cols]
        o_ref[:, cols] = y.astype(o_ref.dtype)

    if final:
        @pl.when(step == 0)
        def _():
            om_ref[...] = o_ref[META_PAD:, :]


def ssd_scan(act, cs4, dt4, cst4, dtt4, tot4, grp, *, rev, ybwd=None, pr=None, pm=None,
             d_skip=None, norm_w=None):
    final = not rev
    nc1, bsz = grp.nc1, grp.bsz
    d = 1 if rev else 0
    cc = (lambda c: nc1 - 1 - c) if rev else (lambda c: c)
    bc_w = SSD_GROUPS * SSD_STATE
    b_blk = (COL_XBC + D_INNER) // bc_w
    ng = SSD_GROUPS
    in_specs = [
        pl.BlockSpec((None, CHUNK, D_INNER), lambda b, c: (b, cc(c), COL_XBC // D_INNER)),
        pl.BlockSpec((None, CHUNK, bc_w), lambda b, c: (b, cc(c), b_blk)),
        pl.BlockSpec((None, CHUNK, bc_w), lambda b, c: (b, cc(c), b_blk + 1)),
        pl.BlockSpec((None, ng, CHUNK, LANE), lambda b, c: (b, d, cc(c), 0)),
        pl.BlockSpec((None, ng, CHUNK, LANE), lambda b, c: (b, d, cc(c), 0)),
        pl.BlockSpec((None, ng, None, SSD_HPG, LANE), lambda b, c: (b, d, cc(c), 0, 0)),
        pl.BlockSpec((None, ng, None, SSD_HPG, LANE), lambda b, c: (b, d, cc(c), 0, 0)),
        pl.BlockSpec((None, ng, None, 8, SSD_TOT_W), lambda b, c: (b, d, cc(c), 0, 0)),
    ]
    args = [act, act, act, cs4, dt4, cst4, dtt4, tot4]
    if final:
        r128, mb = grp.rblk(CHUNK), grp.mblk()
        z_blk = COL_Z // SSD_GW

        def z_real(g):
            return pl.BlockSpec((CHUNK, SSD_GW), lambda b, c: (r128(b) + jnp.maximum(c - 1, 0), z_blk + g))

        def z_meta(g):
            return pl.BlockSpec((N_META, SSD_GW), lambda b, c: (mb(b), z_blk + g))

        in_specs += [
            pl.BlockSpec((None, CHUNK, D_INNER), lambda b, c: (b, c, 0)),
            z_real(0), z_real(1), z_meta(0), z_meta(1),
            pl.BlockSpec((1, D_INNER), lambda b, c: (0, 0)),
            pl.BlockSpec((1, D_INNER), lambda b, c: (0, 0)),
        ]
        args += [ybwd, pr, pr, pm, pm, d_skip, norm_w]
        out_specs = [pl.BlockSpec((CHUNK, D_INNER), lambda b, c: (r128(b) + jnp.maximum(c - 1, 0), 0)),
                     pl.BlockSpec((N_META, D_INNER), lambda b, c: (b, 0))]
        out_shape = [jax.ShapeDtypeStruct((bsz * grp.t, D_INNER), BF16),
                     jax.ShapeDtypeStruct((bsz * N_META, D_INNER), BF16)]
    else:
        out_specs = pl.BlockSpec((None, CHUNK, D_INNER), lambda b, c: (b, cc(c), 0))
        out_shape = jax.ShapeDtypeStruct((bsz, grp.lp, D_INNER), F32)
    return pl.pallas_call(
        functools.partial(_ssd_scan_kernel, rev=rev, final=final),
        grid=(bsz, nc1),
        in_specs=in_specs,
        out_specs=out_specs,
        out_shape=out_shape,
        scratch_shapes=[pltpu.VMEM((SSD_GROUPS, SSD_STATE, SSD_GW), F32)],
        compiler_params=_params(("parallel", "arbitrary")),
        name="ssd_scan_" + ("bwd" if rev else "fwd"),
    )(*args)


def _ret_scan_kernel(*refs, rev, final, nc1):
    if final:
        (dl_ref, qr_ref, qm_ref, kr_ref, km_ref, vr_ref, vm_ref, yb_ref, gr_ref, gm_ref,
         o_ref, om_ref, s_ref, dm_ref, e1_ref, ws_ref, dec_ref) = refs
    else:
        (dl_ref, qr_ref, qm_ref, kr_ref, km_ref, vr_ref, vm_ref,
         o_ref, s_ref, dm_ref, e1_ref, ws_ref, dec_ref) = refs
    step = pl.program_id(1)
    chunk = (nc1 - 1 - step) if rev else step
    is_meta = chunk == 0

    @pl.when(step == 0)
    def _():
        s_ref[...] = jnp.zeros_like(s_ref)
        ti = _iota((CHUNK, CHUNK), 0)
        si = _iota((CHUNK, CHUNK), 1)
        tf = ti.astype(F32)
        for h in range(RET_HEADS):
            x = jnp.full((CHUNK, CHUNK), dl_ref[1 if rev else 0, h], F32)
            lg = jnp.minimum(x, 0.0) - jnp.log1p(jnp.exp(-jnp.abs(x)))
            if rev:
                dm_ref[h] = jnp.exp(jnp.where(si > ti, (si - ti).astype(F32) * lg, NEG))
                e1_ref[h] = jnp.exp((CHUNK - tf) * lg)
                ws_ref[h] = jnp.exp(tf * lg)
            else:
                dm_ref[h] = jnp.exp(jnp.where(si <= ti, (ti - si).astype(F32) * lg, NEG))
                e1_ref[h] = jnp.exp((tf + 1.0) * lg)
                ws_ref[h] = jnp.exp((CHUNK - 1.0 - tf) * lg)
            dec_ref[h] = jnp.exp(CHUNK * lg)

    def pick(r_ref, m_ref):
        width = r_ref.shape[-1]
        return jnp.where(is_meta, _meta_front(m_ref[...], width, BF16), r_ref[...])

    q = pick(qr_ref, qm_ref)
    k = pick(kr_ref, km_ref)
    v = pick(vr_ref, vm_ref)
    if final:
        gate = pick(gr_ref, gm_ref)
    lane = _iota((CHUNK, LANE), 1)
    zero = jnp.zeros((), BF16)
    scale = jnp.asarray(RET_K_DIM ** -0.5, BF16)
    heads = range(RET_HEADS)
    pairs = [slice((h // 2) * LANE, (h // 2 + 1) * LANE) for h in heads]
    cols = [slice(h * LANE, (h + 1) * LANE) for h in heads]
    qh = [jnp.where((lane >= RET_K_DIM) if h % 2 else (lane < RET_K_DIM), q[:, pairs[h]], zero) * scale
          for h in heads]
    sc = [_nt(qh[h], k[:, pairs[h]]) for h in heads]
    inter = [_dot(qh[h], s_ref[h].astype(BF16)) for h in heads]
    for h in heads:
        vw = (v[:, cols[h]].astype(F32) * ws_ref[h]).astype(BF16)
        s_ref[h] = dec_ref[h] * s_ref[h] + _tn(k[:, pairs[h]], vw)
    ys = [_dot((sc[h] * dm_ref[h]).astype(BF16), v[:, cols[h]]) + e1_ref[h] * inter[h] for h in heads]
    for h in heads:
        y = ys[h]
        if final:
            yt = y + yb_ref[:, cols[h]]
            mu = jnp.mean(yt, axis=-1, keepdims=True)
            dev = yt - mu
            var = jnp.mean(dev * dev, axis=-1, keepdims=True)
            y = dev * lax.rsqrt(var + 1e-5) * _silu(gate[:, cols[h]].astype(F32))
        o_ref[:, cols[h]] = y.astype(o_ref.dtype)

    if final:
        @pl.when(step == 0)
        def _():
            om_ref[...] = o_ref[META_PAD:, :]


def ret_scan(pr, pm, decay, grp, *, rev, ybwd=None):
    final = not rev
    nc1, bsz = grp.nc1, grp.bsz
    r128, mb = grp.rblk(CHUNK), grp.mblk()
    cc = (lambda c: nc1 - 1 - c) if rev else (lambda c: c)
    qk_w = RET_HEADS * RET_K_DIM

    def real(col0, width):
        blk = col0 // width
        return pl.BlockSpec((CHUNK, width), lambda b, c: (r128(b) + jnp.maximum(cc(c) - 1, 0), blk))

    def meta(col0, width):
        blk = col0 // width
        return pl.BlockSpec((N_META, width), lambda b, c: (mb(b), blk))

    in_specs = [pl.BlockSpec(memory_space=pltpu.SMEM),
                real(COL_RET_Q, qk_w), meta(COL_RET_Q, qk_w),
                real(COL_RET_K, qk_w), meta(COL_RET_K, qk_w),
                real(COL_RET_V, GROUP_WIDTH), meta(COL_RET_V, GROUP_WIDTH)]
    args = [decay, pr, pm, pr, pm, pr, pm]
    if final:
        in_specs += [pl.BlockSpec((None, CHUNK, GROUP_WIDTH), lambda b, c: (b, c, 0)),
                     real(COL_RET_G, GROUP_WIDTH), meta(COL_RET_G, GROUP_WIDTH)]
        args += [ybwd, pr, pm]
        out_specs = [pl.BlockSpec((CHUNK, GROUP_WIDTH), lambda b, c: (r128(b) + jnp.maximum(c - 1, 0), 0)),
                     pl.BlockSpec((N_META, GROUP_WIDTH), lambda b, c: (b, 0))]
        out_shape = [jax.ShapeDtypeStruct((bsz * grp.t, GROUP_WIDTH), BF16),
                     jax.ShapeDtypeStruct((bsz * N_META, GROUP_WIDTH), BF16)]
    else:
        out_specs = pl.BlockSpec((None, CHUNK, GROUP_WIDTH), lambda b, c: (b, cc(c), 0))
        out_shape = jax.ShapeDtypeStruct((bsz, grp.lp, GROUP_WIDTH), F32)
    table = pltpu.VMEM((RET_HEADS, CHUNK, CHUNK), F32)
    return pl.pallas_call(
        functools.partial(_ret_scan_kernel, rev=rev, final=final, nc1=nc1),
        grid=(bsz, nc1),
        in_specs=in_specs,
        out_specs=out_specs,
        out_shape=out_shape,
        scratch_shapes=[pltpu.VMEM((RET_HEADS, LANE, RET_V_DIM), F32), table, table, table, table],
        compiler_params=_params(("parallel", "arbitrary")),
        name="ret_scan_" + ("bwd" if rev else "fwd"),
    )(*args)


def _alibi_slope_rows(rows_per_head, nrows, g):
    hd = _iota((nrows, 1), 0) // rows_per_head
    s = jnp.full((nrows, 1), 2.0 ** -SWA_REP, F32)
    for r in range(SWA_REP - 1):
        s = jnp.where(hd == r, 2.0 ** -(r + 1), s)
    return s * jnp.where(g == 1, 2.0 ** -SWA_REP, 1.0)


def _sink_rows(sink_ref, rows_per_head, nrows, g):
    hd = _iota((nrows, 1), 0) // rows_per_head
    s = jnp.full((nrows, 1), sink_ref[g * SWA_REP + SWA_REP - 1], F32)
    for r in range(SWA_REP - 1):
        s = jnp.where(hd == r, sink_ref[g * SWA_REP + r], s)
    return s


def _softmax_with_sink(s, sink, vall):
    m = jnp.maximum(jnp.max(s, axis=-1, keepdims=True), sink)
    p = jnp.exp(s - m)
    denom = jnp.sum(p, axis=-1, keepdims=True) + jnp.exp(sink - m)
    return _dot(p.astype(BF16), vall) / denom


def _swa_kernel(sink_ref, q0_ref, q1_ref, kp_ref, ko_ref, kn_ref, vp_ref, vo_ref, vn_ref, km_ref, vm_ref,
                o_ref, bias_ref, s_ref, p_ref, l_ref, *, nb):
    q_refs = (q0_ref, q1_ref)
    n = pl.program_id(1)
    blk = SWA_BLOCK
    nq = SWA_REP * blk
    nkeys = 4 * blk
    groups = range(SWA_KV_HEADS)

    @pl.when(n == 0)
    def _():
        t = _iota((nq, nkeys), 0) % blk
        col = _iota((nq, nkeys), 1)
        rel = col - blk - t
        dist = jnp.abs(rel)
        band = jnp.logical_and(col < 3 * blk, dist <= SWA_WINDOW)
        is_meta = jnp.logical_and(col >= 3 * blk, col < 3 * blk + N_META)
        for g in groups:
            slope = _alibi_slope_rows(blk, nq, g)
            bias_ref[g] = jnp.where(band, -slope * dist.astype(F32), jnp.where(is_meta, 0.0, NEG))

    pad = jnp.zeros((META_PAD, SWA_KV_HEADS * SWA_HEAD_DIM), BF16)
    kall = jnp.concatenate([kp_ref[...], ko_ref[...], kn_ref[...], km_ref[...], pad], axis=0)
    vall = jnp.concatenate([vp_ref[...], vo_ref[...], vn_ref[...], vm_ref[...], pad], axis=0)
    for g, q_ref in enumerate(q_refs):
        q = q_ref[...]
        qs = jnp.concatenate([q[:, r * blk:(r + 1) * blk] for r in range(SWA_REP)], axis=0)
        s_ref[g] = _nt(qs, kall[:, g * LANE:(g + 1) * LANE])
    col = _iota((1, nkeys), 1)
    lo = jnp.where(n == 0, blk, 0)
    hi = jnp.where(n == nb - 1, 2 * blk, 3 * blk)
    outside = jnp.logical_or(col < lo, jnp.logical_and(col >= hi, col < 3 * blk))
    scale = SWA_HEAD_DIM ** -0.5
    outs = []
    for g in groups:
        sink = _sink_rows(sink_ref, blk, nq, g)
        for c in range(nq // SWA_SM_ROWS):
            rows = slice(c * SWA_SM_ROWS, (c + 1) * SWA_SM_ROWS)
            s = jnp.where(outside, NEG, s_ref[g, rows, :] * scale + bias_ref[g, rows, :])
            sk = sink[rows]
            m = jnp.maximum(jnp.max(s, axis=-1, keepdims=True), sk)
            p = jnp.exp(s - m)
            p_ref[g, rows, :] = p.astype(BF16)
            denom = jnp.sum(p, axis=-1, keepdims=True) + jnp.exp(sk - m)
            l_ref[g, rows, :] = jnp.broadcast_to(1.0 / denom, (SWA_SM_ROWS, LANE))
        o = _dot(p_ref[g], vall[:, g * LANE:(g + 1) * LANE]) * l_ref[g]
        outs += [o[r * blk:(r + 1) * blk] for r in range(SWA_REP)]
    o_ref[...] = jnp.concatenate(outs, axis=1).astype(o_ref.dtype)


def swa_real(pr, pm, sink, grp):
    bsz, nb = grp.bsz, grp.t // SWA_BLOCK
    r128, mb = grp.rblk(SWA_BLOCK), grp.mblk()
    kvw = SWA_KV_HEADS * SWA_HEAD_DIM
    qw = SWA_REP * SWA_HEAD_DIM
    q_blk, k_blk, v_blk = COL_SWA_Q // qw, COL_SWA_K // kvw, COL_SWA_V // kvw
    nq, nkeys = SWA_REP * SWA_BLOCK, 4 * SWA_BLOCK

    def kv(col_blk, off):
        return pl.BlockSpec((SWA_BLOCK, kvw),
                            lambda b, n: (r128(b) + jnp.clip(n + off, 0, nb - 1), col_blk))

    in_specs = [pl.BlockSpec(memory_space=pltpu.SMEM),
                pl.BlockSpec((SWA_BLOCK, qw), lambda b, n: (r128(b) + n, q_blk)),
                pl.BlockSpec((SWA_BLOCK, qw), lambda b, n: (r128(b) + n, q_blk + 1)),
                kv(k_blk, -1), kv(k_blk, 0), kv(k_blk, 1),
                kv(v_blk, -1), kv(v_blk, 0), kv(v_blk, 1),
                pl.BlockSpec((N_META, kvw), lambda b, n: (mb(b), k_blk)),
                pl.BlockSpec((N_META, kvw), lambda b, n: (mb(b), v_blk))]
    return pl.pallas_call(
        functools.partial(_swa_kernel, nb=nb),
        grid=(bsz, nb),
        in_specs=in_specs,
        out_specs=pl.BlockSpec((SWA_BLOCK, GROUP_WIDTH), lambda b, n: (b * nb + n, 0)),
        out_shape=jax.ShapeDtypeStruct((bsz * grp.t, GROUP_WIDTH), BF16),
        scratch_shapes=[pltpu.VMEM((SWA_KV_HEADS, nq, nkeys), F32),
                        pltpu.VMEM((SWA_KV_HEADS, nq, nkeys), F32),
                        pltpu.VMEM((SWA_KV_HEADS, nq, nkeys), BF16),
                        pltpu.VMEM((SWA_KV_HEADS, nq, LANE), F32)],
        compiler_params=_params(("parallel", "arbitrary")),
        name="swa_real",
    )(sink, pr, pr, pr, pr, pr, pr, pr, pr, pm, pm)


def _swa_meta_kernel(sink_ref, q_ref, k0_ref, v0_ref, km_ref, vm_ref, o_ref):
    g = pl.program_id(1)
    nq = SWA_REP * N_META
    nkeys = 2 * SWA_BLOCK
    q = q_ref[...]
    qs = jnp.concatenate([q[:, r * LANE:(r + 1) * LANE] for r in range(SWA_REP)], axis=0)
    pad = jnp.zeros((META_PAD, SWA_HEAD_DIM), BF16)
    kall = jnp.concatenate([k0_ref[...], km_ref[...], pad], axis=0)
    vall = jnp.concatenate([v0_ref[...], vm_ref[...], pad], axis=0)
    i = _iota((nq, nkeys), 0) % N_META
    col = _iota((nq, nkeys), 1)
    mdist = N_META + col - i
    near = jnp.logical_and(col < SWA_BLOCK, mdist <= SWA_WINDOW)
    is_meta = jnp.logical_and(col >= SWA_BLOCK, col < SWA_BLOCK + N_META)
    slope = _alibi_slope_rows(N_META, nq, g)
    bias = jnp.where(near, -slope * mdist.astype(F32), jnp.where(is_meta, 0.0, NEG))
    s = _nt(qs, kall) * (SWA_HEAD_DIM ** -0.5) + bias
    o = _softmax_with_sink(s, _sink_rows(sink_ref, N_META, nq, g), vall)
    o_ref[...] = jnp.concatenate([o[r * N_META:(r + 1) * N_META] for r in range(SWA_REP)],
                                 axis=1).astype(o_ref.dtype)


def swa_meta(pr, pm, sink, grp):
    bsz = grp.bsz
    r128, mb = grp.rblk(SWA_BLOCK), grp.mblk()
    qw = SWA_REP * SWA_HEAD_DIM
    q_blk, k_blk, v_blk = COL_SWA_Q // qw, COL_SWA_K // LANE, COL_SWA_V // LANE
    in_specs = [pl.BlockSpec(memory_space=pltpu.SMEM),
                pl.BlockSpec((N_META, qw), lambda b, g: (mb(b), q_blk + g)),
                pl.BlockSpec((SWA_BLOCK, LANE), lambda b, g: (r128(b), k_blk + g)),
                pl.BlockSpec((SWA_BLOCK, LANE), lambda b, g: (r128(b), v_blk + g)),
                pl.BlockSpec((N_META, LANE), lambda b, g: (mb(b), k_blk + g)),
                pl.BlockSpec((N_META, LANE), lambda b, g: (mb(b), v_blk + g))]
    return pl.pallas_call(
        _swa_meta_kernel,
        grid=(bsz, SWA_KV_HEADS),
        in_specs=in_specs,
        out_specs=pl.BlockSpec((N_META, qw), lambda b, g: (b, g)),
        out_shape=jax.ShapeDtypeStruct((bsz * N_META, GROUP_WIDTH), BF16),
        compiler_params=_params(("parallel", "parallel")),
        name="swa_meta",
    )(sink, pm, pr, pr, pm, pm)


NA_KROWS = 3 * NA_QROWS
NA_RPB_H = 2 * NA_KH - 1
NA_RPB_W = 2 * NA_KW - 1
NA_VARIANTS = 3
NA_SM_ROWS = 32
NA_HPS = 4


def _na_row_ok(variant, i, j):
    if variant == 0:
        return NA_QROWS <= j < NA_QROWS + NA_KH
    if variant == 1:
        return i <= j < i + NA_KH
    return j < NA_KH


def _na_bias_kernel(rpb_ref, o_ref):
    h = pl.program_id(0)
    base = h * (NA_RPB_H * NA_RPB_W)
    qc = _iota((GRID_W, LANE), 0)
    lane = _iota((GRID_W, LANE), 1)
    lo_half = lane < GRID_W
    kc = jnp.where(lo_half, lane, lane - GRID_W)
    d = kc - qc + (NA_KW - 1)
    col_start = jnp.clip(qc - NA_KW // 2, 0, GRID_W - NA_KW)
    col_ok = jnp.logical_and(kc >= col_start, kc < col_start + NA_KW)
    neg = jnp.full((GRID_W, LANE), NEG, F32)
    for a in range(NA_RPB_H - 1):
        val = jnp.zeros((GRID_W, LANE), F32)
        for b in range(NA_RPB_W):
            s_lo = rpb_ref[base + a * NA_RPB_W + b]
            s_hi = rpb_ref[base + (a + 1) * NA_RPB_W + b]
            val = jnp.where(d == b, jnp.where(lo_half, s_lo, s_hi), val)
        tile = jnp.where(col_ok, val, neg)
        for jp in range(NA_KROWS // 2):
            i = 2 * jp + (NA_QROWS - 1) - a
            if not 0 <= i < NA_QROWS:
                continue
            for v in range(NA_VARIANTS):
                ok_lo, ok_hi = _na_row_ok(v, i, 2 * jp), _na_row_ok(v, i, 2 * jp + 1)
                if ok_lo and ok_hi:
                    blk = tile
                elif ok_lo:
                    blk = jnp.where(lo_half, tile, neg)
                elif ok_hi:
                    blk = jnp.where(lo_half, neg, tile)
                else:
                    blk = neg
                o_ref[v, i * GRID_W:(i + 1) * GRID_W, jp * LANE:(jp + 1) * LANE] = blk
    meta_cols = jnp.where(_iota((NA_QB, LANE), 1) < N_META, 0.0, NEG)
    for v in range(NA_VARIANTS):
        o_ref[v, :, 3 * NA_QB:] = meta_cols


def na_bias(rpb):
    return pl.pallas_call(
        _na_bias_kernel,
        grid=(NA_HEADS,),
        in_specs=[pl.BlockSpec(memory_space=pltpu.SMEM)],
        out_specs=pl.BlockSpec((NA_VARIANTS, None, NA_QB, NA_KEYS), lambda h: (0, h, 0, 0)),
        out_shape=jax.ShapeDtypeStruct((NA_VARIANTS, NA_HEADS, NA_QB, NA_KEYS), F32),
        compiler_params=_params(("parallel",)),
        name="na_bias",
    )(rpb.astype(F32).reshape(-1))


def _na_kernel(q_ref, kp_ref, ko_ref, kn_ref, vp_ref, vo_ref, vn_ref, km_ref, vm_ref,
               bias_ref, o_ref, s_ref, p_ref, l_ref):
    q = q_ref[...] * jnp.asarray(NA_HEAD_DIM ** -0.5, BF16)
    width = NA_HPS * NA_HEAD_DIM
    pad = jnp.zeros((META_PAD, width), BF16)
    kall = jnp.concatenate([kp_ref[...], ko_ref[...], kn_ref[...], km_ref[...], pad], axis=0)
    vall = jnp.concatenate([vp_ref[...], vo_ref[...], vn_ref[...], vm_ref[...], pad], axis=0)
    zero = jnp.zeros((), BF16)
    qlane = _iota((NA_QB, LANE), 1)
    vlane = _iota((NA_KEYS, LANE), 1)
    tiles = [slice((hh // 2) * LANE, (hh // 2 + 1) * LANE) for hh in range(NA_HPS)]
    for hh in range(NA_HPS):
        qsel = (qlane >= NA_HEAD_DIM) if hh % 2 else (qlane < NA_HEAD_DIM)
        s_ref[hh] = _nt(jnp.where(qsel, q[:, tiles[hh]], zero), kall[:, tiles[hh]])
    outs = [None] * (NA_HPS // 2)
    for hh in range(NA_HPS):
        for c in range(NA_QB // NA_SM_ROWS):
            rows = slice(c * NA_SM_ROWS, (c + 1) * NA_SM_ROWS)
            s = s_ref[hh, rows, :] + bias_ref[hh, rows, :]
            m = jnp.max(s, axis=-1, keepdims=True)
            p = jnp.exp(s - m)
            p_ref[hh, rows, :] = p.astype(BF16)
            l_ref[hh, rows, :] = jnp.broadcast_to(1.0 / jnp.sum(p, axis=-1, keepdims=True),
                                                  (NA_SM_ROWS, LANE))
        vsel = (vlane >= NA_HEAD_DIM) if hh % 2 else (vlane < NA_HEAD_DIM)
        o = _dot(p_ref[hh], jnp.where(vsel, vall[:, tiles[hh]], zero)) * l_ref[hh]
        outs[hh // 2] = o if outs[hh // 2] is None else outs[hh // 2] + o
    o_ref[...] = jnp.concatenate(outs, axis=1).astype(o_ref.dtype)


def na_real(pr, pm, bias, grp):
    bsz, nqb = grp.bsz, grp.t // NA_QB
    rq, mb = grp.rblk(NA_QB), grp.mblk()
    width = NA_HPS * NA_HEAD_DIM
    q_blk, k_blk, v_blk = COL_NA_Q // width, COL_NA_K // width, COL_NA_V // width

    def kv(col_blk, off):
        return pl.BlockSpec((NA_QB, width),
                            lambda p, b, n: (rq(b) + jnp.clip(n + off, 0, nqb - 1), col_blk + p))

    in_specs = [pl.BlockSpec((NA_QB, width), lambda p, b, n: (rq(b) + n, q_blk + p)),
                kv(k_blk, -1), kv(k_blk, 0), kv(k_blk, 1),
                kv(v_blk, -1), kv(v_blk, 0), kv(v_blk, 1),
                pl.BlockSpec((N_META, width), lambda p, b, n: (mb(b), k_blk + p)),
                pl.BlockSpec((N_META, width), lambda p, b, n: (mb(b), v_blk + p)),
                pl.BlockSpec((None, NA_HPS, NA_QB, NA_KEYS),
                             lambda p, b, n: (jnp.where(n == 0, 0, jnp.where(n == nqb - 1, 2, 1)), p, 0, 0))]
    return pl.pallas_call(
        _na_kernel,
        grid=(NA_HEADS // NA_HPS, bsz, nqb),
        in_specs=in_specs,
        out_specs=pl.BlockSpec((NA_QB, width), lambda p, b, n: (b * nqb + n, p)),
        out_shape=jax.ShapeDtypeStruct((bsz * grp.t, GROUP_WIDTH), BF16),
        scratch_shapes=[pltpu.VMEM((NA_HPS, NA_QB, NA_KEYS), F32),
                        pltpu.VMEM((NA_HPS, NA_QB, NA_KEYS), BF16),
                        pltpu.VMEM((NA_HPS, NA_QB, LANE), F32)],
        compiler_params=_params(("parallel", "parallel", "arbitrary")),
        name="na_real",
    )(pr, pr, pr, pr, pr, pr, pr, pm, pm, bias)


def _na_meta_kernel(q_ref, k_ref, v_ref, km_ref, vm_ref, o_ref):
    nwin = NA_KH * GRID_W
    nkeys = nwin + CHUNK
    q = q_ref[...]
    pad = jnp.zeros((META_PAD, LANE), BF16)
    kall = jnp.concatenate([k_ref[...], km_ref[...], pad], axis=0)
    vall = jnp.concatenate([v_ref[...], vm_ref[...], pad], axis=0)
    col = _iota((N_META, nkeys), 1)
    in_win = jnp.logical_and(col < nwin, col % GRID_W < NA_KW)
    is_meta = jnp.logical_and(col >= nwin, col < nwin + N_META)
    bias = jnp.where(jnp.logical_or(in_win, is_meta), 0.0, NEG)
    zero = jnp.zeros((), BF16)
    qlane = _iota((N_META, LANE), 1)
    vlane = _iota((nkeys, LANE), 1)
    out = jnp.zeros((N_META, LANE), F32)
    for hh in range(2):
        qsel = (qlane >= NA_HEAD_DIM) if hh else (qlane < NA_HEAD_DIM)
        vsel = (vlane >= NA_HEAD_DIM) if hh else (vlane < NA_HEAD_DIM)
        s = _nt(jnp.where(qsel, q, zero), kall) * (NA_HEAD_DIM ** -0.5) + bias
        m = jnp.max(s, axis=-1, keepdims=True)
        p = jnp.exp(s - m)
        denom = jnp.sum(p, axis=-1, keepdims=True)
        out = out + _dot(p.astype(BF16), jnp.where(vsel, vall, zero)) / denom
    o_ref[...] = out.astype(o_ref.dtype)


def na_meta(pr, pm, grp):
    bsz = grp.bsz
    nwin = NA_KH * GRID_W
    rw, mb = grp.rblk(nwin), grp.mblk()
    q_blk, k_blk, v_blk = COL_NA_Q // LANE, COL_NA_K // LANE, COL_NA_V // LANE
    in_specs = [pl.BlockSpec((N_META, LANE), lambda b, p: (mb(b), q_blk + p)),
                pl.BlockSpec((nwin, LANE), lambda b, p: (rw(b), k_blk + p)),
                pl.BlockSpec((nwin, LANE), lambda b, p: (rw(b), v_blk + p)),
                pl.BlockSpec((N_META, LANE), lambda b, p: (mb(b), k_blk + p)),
                pl.BlockSpec((N_META, LANE), lambda b, p: (mb(b), v_blk + p))]
    return pl.pallas_call(
        _na_meta_kernel,
        grid=(bsz, NA_HEADS // 2),
        in_specs=in_specs,
        out_specs=pl.BlockSpec((N_META, LANE), lambda b, p: (b, p)),
        out_shape=jax.ShapeDtypeStruct((bsz * N_META, GROUP_WIDTH), BF16),
        compiler_params=_params(("parallel", "parallel")),
        name="na_meta",
    )(pm, pr, pr, pm, pm)


def mix_group(pr, pm, dtr, dtm, lw, grp, need_meta):
    act, cs4, dt4, cst4, dtt4, tot4 = ssd_pre(pr, pm, dtr, dtm, lw["conv_w"], lw["conv_b"],
                                               lw["bias4"], lw["alog4"], grp)
    yb = ssd_scan(act, cs4, dt4, cst4, dtt4, tot4, grp, rev=True)
    y_ssd, ym_ssd = ssd_scan(act, cs4, dt4, cst4, dtt4, tot4, grp, rev=False, ybwd=yb, pr=pr, pm=pm,
                             d_skip=lw["d_skip"], norm_w=lw["ssd_norm_w"])
    rb = ret_scan(pr, pm, lw["ret_decay"], grp, rev=True)
    y_ret, ym_ret = ret_scan(pr, pm, lw["ret_decay"], grp, rev=False, ybwd=rb)
    real = [y_ssd, swa_real(pr, pm, lw["sink"], grp), na_real(pr, pm, lw["na_bias"], grp), y_ret]
    meta = None
    if need_meta:
        meta = [ym_ssd, swa_meta(pr, pm, lw["sink"], grp), na_meta(pr, pm, grp), ym_ret]
    return real, meta


def _layer_weights(i, w_in, ssd_conv_w, ssd_conv_b, ssd_dt_bias, ssd_a_log, ssd_d, ssd_norm_w,
                   swa_sink, na_rpb, ret_decay, w_out, w_up, w_down):
    w_main, w_dt = cast_w_in(w_in, i)

    def per_dir_group(v):
        v4 = v.astype(F32).reshape(2 * SSD_GROUPS, 1, SSD_HPG)
        return jnp.pad(v4, ((0, 0), (0, 0), (0, LANE - SSD_HPG)))

    return dict(
        w_main=w_main, w_dt=w_dt,
        conv_w=jnp.pad(ssd_conv_w[i].astype(F32), ((0, 8 - SSD_CONV_W), (0, 0))),
        conv_b=ssd_conv_b[i].astype(F32).reshape(1, SSD_CONV_DIM),
        bias4=per_dir_group(ssd_dt_bias[i]), alog4=per_dir_group(ssd_a_log[i]),
        d_skip=jnp.repeat(ssd_d[i].astype(F32), SSD_HEAD_DIM).reshape(1, D_INNER),
        ssd_norm_w=ssd_norm_w[i].astype(F32).reshape(1, D_INNER),
        sink=swa_sink[i].astype(F32), na_bias=na_bias(na_rpb[i]),
        ret_decay=ret_decay[i].astype(F32),
        w_out=cast_bf16(w_out, i), w_up=cast_bf16(w_up, i), w_down=cast_bf16(w_down, i),
    )


TM_REAL = 1024
TN = 512
TN_BF16 = 1024
TK_DOWN = 4096
TM_NORM = 256


def _mlp(x, tm, lw, norm_w):
    n2 = rmsnorm(x, norm_w, BF16, min(tm, TM_NORM))
    u = matmul(n2, lw["w_up"], tm=tm, tn=TN_BF16, tk=D_MODEL, out_dtype=BF16, epilogue="relu2")
    return matmul(u, lw["w_down"], tm=tm, tn=TN, tk=TK_DOWN, out_dtype=F32,
                  epilogue="residual", residual=x)


def kernel(x_prompt, x_sample, meta_tokens, norm1_w, w_in, ssd_conv_w, ssd_conv_b, ssd_dt_bias,
           ssd_a_log, ssd_d, ssd_norm_w, swa_sink, na_rpb, ret_decay, w_out, norm2_w, w_up, w_down,
           final_norm_w):
    d = D_MODEL
    inputs = [x_prompt, x_sample]
    groups, mbase = [], 0
    for x in inputs:
        groups.append(Group(mbase, x.shape[0], x.shape[1]))
        mbase += x.shape[0] * N_META
    n_meta = mbase
    xs = [x.reshape(-1, d).astype(F32) for x in inputs]
    xm = jnp.tile(meta_tokens.astype(F32), (n_meta // N_META, 1))

    for i in range(DEPTH):
        last = i == DEPTH - 1
        lw = _layer_weights(i, w_in, ssd_conv_w, ssd_conv_b, ssd_dt_bias, ssd_a_log, ssd_d,
                            ssd_norm_w, swa_sink, na_rpb, ret_decay, w_out, w_up, w_down)
        nm = rmsnorm(xm, norm1_w[i], BF16, n_meta)
        pm = matmul(nm, lw["w_main"], tm=n_meta, tn=TN_BF16, tk=d, out_dtype=BF16)
        dtm = matmul(nm, lw["w_dt"], tm=n_meta, tn=LANE, tk=d, out_dtype=F32)
        meta_parts = []
        for gi, grp in enumerate(groups):
            nr = rmsnorm(xs[gi], norm1_w[i], BF16, TM_NORM)
            pr = matmul(nr, lw["w_main"], tm=TM_REAL, tn=TN_BF16, tk=d, out_dtype=BF16)
            dtr = matmul(nr, lw["w_dt"], tm=TM_REAL, tn=LANE, tk=d, out_dtype=F32)
            real, meta = mix_group(pr, pm, dtr, dtm, lw, grp, need_meta=not last)
            meta_parts.append(meta)
            h = matmul(real, lw["w_out"], tm=TM_REAL, tn=TN, tk=GROUP_WIDTH, out_dtype=F32,
                       epilogue="residual", residual=xs[gi], alias=i > 0)
            xs[gi] = _mlp(h, TM_REAL, lw, norm2_w[i])
        if not last:
            mixed_m = [jnp.concatenate([m[k] for m in meta_parts], axis=0) for k in range(4)]
            hm = matmul(mixed_m, lw["w_out"], tm=n_meta, tn=TN, tk=GROUP_WIDTH, out_dtype=F32,
                        epilogue="residual", residual=xm)
            xm = _mlp(hm, n_meta, lw, norm2_w[i])

    outs = [rmsnorm(x, final_norm_w, F32, TM_NORM).reshape(inp.shape)
            for x, inp in zip(xs, inputs)]
    return tuple(outs)
```

```python
import functools
import numpy as np
import jax
import jax.numpy as jnp
from jax import lax
from jax.experimental import pallas as pl
from jax.experimental.pallas import tpu as pltpu

F32 = jnp.float32
BF16 = jnp.bfloat16
HIGHEST = lax.Precision.HIGHEST

D_MODEL = 4096
DEPTH = 2
N_META = 16
GRID_W = 64
GROUP_WIDTH = D_MODEL // 4
D_FF = 4 * D_MODEL
EPS = 1e-6
CHUNK = 128
META_PAD = CHUNK - N_META

SSD_HEAD_DIM = 64
SSD_HEADS = 16
SSD_GROUPS = 2
SSD_HPG = 8
SSD_STATE = 128
SSD_CONV_W = 5
D_INNER = GROUP_WIDTH
SSD_CONV_DIM = D_INNER + 2 * SSD_GROUPS * SSD_STATE
SSD_GW = SSD_HPG * SSD_HEAD_DIM
SSD_CONV_TILE = 256
SSD_TOT_W = 128 + SSD_GW + SSD_HPG * 128

SWA_HEAD_DIM = 128
SWA_HEADS = 8
SWA_KV_HEADS = 2
SWA_REP = SWA_HEADS // SWA_KV_HEADS
SWA_WINDOW = 128
SWA_BLOCK = 128
SWA_SM_ROWS = 32

NA_HEAD_DIM = 64
NA_HEADS = 16
NA_KH = 8
NA_KW = 16
NA_QROWS = 4
NA_QB = NA_QROWS * GRID_W
NA_KEYS = 3 * NA_QB + CHUNK

RET_HEADS = 8
RET_K_DIM = 64
RET_V_DIM = 128

LANE = 128
COL_XBC = 0
COL_Z = 1536
COL_SWA_Q = 2560
COL_SWA_K = 3584
COL_SWA_V = 3840
COL_NA_Q = 4096
COL_NA_K = 5120
COL_NA_V = 6144
COL_RET_Q = 7168
COL_RET_K = 7680
COL_RET_V = 8192
COL_RET_G = 9216
P_COLS = 10240

MXU_WIDTH = 256
NEG = -1e30
VMEM_LIMIT_BYTES = 56 * 1024 * 1024


def _params(sem):
    return pltpu.CompilerParams(dimension_semantics=sem, vmem_limit_bytes=VMEM_LIMIT_BYTES)


def _nt(a, b):
    return lax.dot_general(a, b, (((1,), (1,)), ((), ())), preferred_element_type=F32)


def _tn(a, b):
    return lax.dot_general(a, b, (((0,), (0,)), ((), ())), preferred_element_type=F32)


def _dot(a, b):
    return jnp.dot(a, b, preferred_element_type=F32)


def _dot_exact(a, b):
    return jnp.dot(a, b, preferred_element_type=F32, precision=HIGHEST)


def _silu(x):
    return x * jax.nn.sigmoid(x)


def _softplus(x):
    return jnp.maximum(x, 0.0) + jnp.log1p(jnp.exp(-jnp.abs(x)))


def _iota(shape, dim):
    return lax.broadcasted_iota(jnp.int32, shape, dim)


def _rmsnorm_kernel(x_ref, w_ref, o_ref):
    x = x_ref[...]
    ms = jnp.mean(x * x, axis=-1, keepdims=True)
    o_ref[...] = (x * lax.rsqrt(ms + EPS) * w_ref[...]).astype(o_ref.dtype)


def rmsnorm(x, w, out_dtype, tm):
    m, d = x.shape
    return pl.pallas_call(
        _rmsnorm_kernel,
        grid=(m // tm,),
        in_specs=[pl.BlockSpec((tm, d), lambda i: (i, 0)),
                  pl.BlockSpec((1, d), lambda i: (0, 0))],
        out_specs=pl.BlockSpec((tm, d), lambda i: (i, 0)),
        out_shape=jax.ShapeDtypeStruct((m, d), out_dtype),
        compiler_params=_params(("parallel",)),
        name="rmsnorm",
    )(x, w.reshape(1, d).astype(F32))


CAST_BLOCK_BYTES = 8 * 1024 * 1024


def _cast_kernel(x_ref, o_ref):
    o_ref[...] = x_ref[...].astype(o_ref.dtype)


def cast_bf16(w, layer):
    _, r, c = w.shape
    rows = min(r, CAST_BLOCK_BYTES // (4 * c))
    assert r % rows == 0 and rows % 16 == 0
    return pl.pallas_call(
        _cast_kernel,
        grid=(r // rows,),
        in_specs=[pl.BlockSpec((None, rows, c), lambda i: (layer, i, 0))],
        out_specs=pl.BlockSpec((rows, c), lambda i: (i, 0)),
        out_shape=jax.ShapeDtypeStruct((r, c), BF16),
        compiler_params=_params(("parallel",)),
        name="cast_bf16",
    )(w.astype(F32))


def _cast_w_in_kernel(x_ref, main_ref, dt_ref):
    x0, dt0 = D_INNER, D_INNER + SSD_CONV_DIM
    rest0 = dt0 + 2 * SSD_HEADS
    main_ref[:, COL_XBC:COL_Z] = x_ref[:, x0:dt0].astype(BF16)
    main_ref[:, COL_Z:COL_SWA_Q] = x_ref[:, 0:x0].astype(BF16)
    main_ref[:, COL_SWA_Q:] = x_ref[:, rest0:].astype(BF16)
    lane = _iota((x_ref.shape[0], LANE), 1)
    dt_ref[...] = jnp.where(lane < 2 * SSD_HEADS, x_ref[:, dt0:dt0 + LANE], 0.0).astype(BF16)


def cast_w_in(w_in, layer):
    _, r, c = w_in.shape
    rows = 128
    return pl.pallas_call(
        _cast_w_in_kernel,
        grid=(r // rows,),
        in_specs=[pl.BlockSpec((None, rows, c), lambda i: (layer, i, 0))],
        out_specs=[pl.BlockSpec((rows, P_COLS), lambda i: (i, 0)),
                   pl.BlockSpec((rows, LANE), lambda i: (i, 0))],
        out_shape=[jax.ShapeDtypeStruct((r, P_COLS), BF16), jax.ShapeDtypeStruct((r, LANE), BF16)],
        compiler_params=_params(("parallel",)),
        name="cast_w_in",
    )(w_in.astype(F32))


def _mm_kernel(*refs, n_in, epilogue, nk):
    x_refs, w_refs, rest = refs[:n_in], refs[n_in:2 * n_in], refs[2 * n_in:]
    if epilogue == "residual":
        r_ref, o_ref = rest
    else:
        (o_ref,) = rest
    if nk > 1:
        @pl.when(pl.program_id(2) == 0)
        def _():
            o_ref[...] = r_ref[...]

    tn = o_ref.shape[1]
    for c0 in range(0, tn, MXU_WIDTH):
        cols = slice(c0, min(c0 + MXU_WIDTH, tn))
        part = _dot(x_refs[0][...], w_refs[0][:, cols])
        for x_ref, w_ref in zip(x_refs[1:], w_refs[1:]):
            part = part + _dot(x_ref[...], w_ref[:, cols])
        if nk > 1:
            o_ref[:, cols] += part
        else:
            if epilogue == "relu2":
                part = jnp.square(jnp.maximum(part, 0.0))
            elif epilogue == "residual":
                part = part + r_ref[:, cols]
            o_ref[:, cols] = part.astype(o_ref.dtype)


def matmul(xs, w, *, tm, tn, tk, out_dtype, epilogue="none", residual=None, alias=True):
    xs = list(xs) if isinstance(xs, (list, tuple)) else [xs]
    n_in = len(xs)
    m, kc = xs[0].shape
    n = w.shape[1]
    if n_in > 1:
        assert tk == kc and w.shape[0] == n_in * kc
        nk = 1
        w_specs = [pl.BlockSpec((kc, tn), functools.partial(lambda i, j, k, c: (c, j), c=c))
                   for c in range(n_in)]
    else:
        nk = kc // tk
        w_specs = [pl.BlockSpec((tk, tn), lambda i, j, k: (k, j))]
    assert nk == 1 or (epilogue == "residual" and out_dtype == F32)
    in_specs = [pl.BlockSpec((tm, tk), lambda i, j, k: (i, k)) for _ in xs] + w_specs
    args = xs + [w] * n_in
    aliases = {}
    if epilogue == "residual":
        in_specs.append(pl.BlockSpec((tm, tn), lambda i, j, k: (i, j)))
        args.append(residual)
        if alias:
            aliases = {2 * n_in: 0}
    return pl.pallas_call(
        functools.partial(_mm_kernel, n_in=n_in, epilogue=epilogue, nk=nk),
        grid=(m // tm, n // tn, nk),
        in_specs=in_specs,
        out_specs=pl.BlockSpec((tm, tn), lambda i, j, k: (i, j)),
        out_shape=jax.ShapeDtypeStruct((m, n), out_dtype),
        input_output_aliases=aliases,
        compiler_params=_params(("parallel", "parallel", "arbitrary")),
        name="matmul_" + epilogue,
    )(*args)


class Group:
    def __init__(self, mbase, bsz, t):
        self.mbase = mbase
        self.bsz = bsz
        self.t = t
        self.nc = t // CHUNK
        self.nc1 = self.nc + 1
        self.lp = self.nc1 * CHUNK

    def rblk(self, rows):
        per = self.t // rows
        return lambda b: b * per

    def mblk(self):
        first = self.mbase // N_META
        return lambda b: first + b


def _meta_front(m_val, width, dtype):
    return jnp.concatenate([jnp.zeros((META_PAD, width), dtype), m_val], axis=0)


def _ssd_pre_kernel(cur_ref, prev_ref, next_ref, meta_ref, dtr_ref, dtm_ref, cw_ref, cb_ref,
                    bias4_ref, alog4_ref,
                    act_ref, cs_ref, dtv_ref, cst_ref, dtt_ref, tot_ref, ext_ref, *, nc):
    c = pl.program_id(1)
    is_meta = c == 0
    half = (SSD_CONV_W - 1) // 2
    row = _iota((CHUNK, 1), 0)
    valid = jnp.logical_or(c > 0, row >= META_PAD)
    ext_ref[0:N_META, :] = prev_ref[...].astype(F32)
    ext_ref[N_META:N_META + CHUNK, :] = cur_ref[...].astype(F32)
    ext_ref[N_META + CHUNK:, :] = next_ref[...].astype(F32)

    @pl.when(is_meta)
    def _():
        ext_ref[0:N_META + META_PAD, :] = jnp.zeros((N_META + META_PAD, SSD_CONV_DIM), F32)
        ext_ref[N_META + META_PAD:N_META + CHUNK, :] = meta_ref[...].astype(F32)

    @pl.when(c == 1)
    def _():
        ext_ref[0:N_META, :] = meta_ref[...].astype(F32)

    @pl.when(c == nc)
    def _():
        ext_ref[N_META + CHUNK:, :] = jnp.zeros((N_META, SSD_CONV_DIM), F32)

    for c0 in range(0, SSD_CONV_DIM, SSD_CONV_TILE):
        cols = slice(c0, c0 + SSD_CONV_TILE)
        acc = jnp.broadcast_to(cb_ref[:, cols], (CHUNK, SSD_CONV_TILE))
        for j in range(SSD_CONV_W):
            acc = acc + cw_ref[j:j + 1, cols] * ext_ref[pl.ds(N_META - half + j, CHUNK), cols]
        act_ref[:, cols] = jnp.where(valid, _silu(acc), 0.0).astype(act_ref.dtype)

    dtx = jnp.where(is_meta, _meta_front(dtm_ref[...], LANE, F32), dtr_ref[...])
    lane = _iota((CHUNK, LANE), 1)
    keep = jnp.logical_and(valid, lane < SSD_HPG)
    ti = _iota((CHUNK, CHUNK), 0)
    si = _iota((CHUNK, CHUNK), 1)
    tri = (si <= ti).astype(F32)
    for k in range(2 * SSD_GROUPS):
        x = dtx if k == 0 else pltpu.roll(dtx, LANE - SSD_HPG * k, axis=1)
        dt = jnp.where(keep, _softplus(x + bias4_ref[k]), 0.0)
        la = dt * (-jnp.exp(alog4_ref[k]))
        incl = _dot_exact(tri, la)
        cs = incl if k < SSD_GROUPS else incl - la
        cs_ref[k] = cs
        dtv_ref[k] = dt
        cst_ref[k] = jnp.transpose(cs)[0:SSD_HPG, :]
        dtt_ref[k] = jnp.transpose(dt)[0:SSD_HPG, :]
        last = jnp.broadcast_to(incl[CHUNK - 1:CHUNK, :], (8, LANE))
        tiles = [jnp.broadcast_to(last[:, h:h + 1], (8, LANE)) for h in range(SSD_HPG)]
        halves = [jnp.where(lane[0:8] < SSD_HEAD_DIM, tiles[2 * p], tiles[2 * p + 1])
                  for p in range(SSD_HPG // 2)]
        tot_ref[k] = jnp.concatenate([last] + halves + tiles, axis=1)


def ssd_pre(pr, pm, dtr, dtm, conv_w, conv_b, bias4, alog4, grp):
    nc, nc1, bsz = grp.nc, grp.nc1, grp.bsz
    w = SSD_CONV_DIM
    r128, r16, mb = grp.rblk(CHUNK), grp.rblk(N_META), grp.mblk()
    per16 = CHUNK // N_META
    ng = 2 * SSD_GROUPS
    in_specs = [
        pl.BlockSpec((CHUNK, w), lambda b, c: (r128(b) + jnp.maximum(c - 1, 0), 0)),
        pl.BlockSpec((N_META, w), lambda b, c: (r16(b) + jnp.maximum((c - 1) * per16 - 1, 0), 0)),
        pl.BlockSpec((N_META, w), lambda b, c: (r16(b) + jnp.minimum(c * per16, nc * per16 - 1), 0)),
        pl.BlockSpec((N_META, w), lambda b, c: (mb(b), 0)),
        pl.BlockSpec((CHUNK, LANE), lambda b, c: (r128(b) + jnp.maximum(c - 1, 0), 0)),
        pl.BlockSpec((N_META, LANE), lambda b, c: (mb(b), 0)),
        pl.BlockSpec((8, w), lambda b, c: (0, 0)),
        pl.BlockSpec((1, w), lambda b, c: (0, 0)),
        pl.BlockSpec((ng, 1, LANE), lambda b, c: (0, 0, 0)),
        pl.BlockSpec((ng, 1, LANE), lambda b, c: (0, 0, 0)),
    ]
    out_specs = [
        pl.BlockSpec((None, CHUNK, w), lambda b, c: (b, c, 0)),
        pl.BlockSpec((None, ng, CHUNK, LANE), lambda b, c: (b, 0, c, 0)),
        pl.BlockSpec((None, ng, CHUNK, LANE), lambda b, c: (b, 0, c, 0)),
        pl.BlockSpec((None, ng, None, SSD_HPG, LANE), lambda b, c: (b, 0, c, 0, 0)),
        pl.BlockSpec((None, ng, None, SSD_HPG, LANE), lambda b, c: (b, 0, c, 0, 0)),
        pl.BlockSpec((None, ng, None, 8, SSD_TOT_W), lambda b, c: (b, 0, c, 0, 0)),
    ]
    out_shape = [
        jax.ShapeDtypeStruct((bsz, grp.lp, w), BF16),
        jax.ShapeDtypeStruct((bsz, ng, grp.lp, LANE), F32),
        jax.ShapeDtypeStruct((bsz, ng, grp.lp, LANE), F32),
        jax.ShapeDtypeStruct((bsz, ng, nc1, SSD_HPG, LANE), F32),
        jax.ShapeDtypeStruct((bsz, ng, nc1, SSD_HPG, LANE), F32),
        jax.ShapeDtypeStruct((bsz, ng, nc1, 8, SSD_TOT_W), F32),
    ]
    return pl.pallas_call(
        functools.partial(_ssd_pre_kernel, nc=nc),
        grid=(bsz, nc1),
        in_specs=in_specs, out_specs=out_specs, out_shape=out_shape,
        scratch_shapes=[pltpu.VMEM((CHUNK + 2 * N_META, w), F32)],
        compiler_params=_params(("parallel", "arbitrary")),
        name="ssd_pre",
    )(pr, pr, pr, pm, dtr, dtm, conv_w, conv_b, bias4, alog4)


def _widen_heads(x):
    sel = (_iota((LANE, SSD_GW), 0) == _iota((LANE, SSD_GW), 1) // SSD_HEAD_DIM).astype(BF16)
    hi = x.astype(BF16)
    r1 = x - hi.astype(F32)
    mid = r1.astype(BF16)
    lo = (r1 - mid.astype(F32)).astype(BF16)
    return _dot(jnp.concatenate([hi, mid, lo], axis=1), jnp.concatenate([sel, sel, sel], axis=0))


def _ssd_scan_kernel(*refs, rev, final):
    if final:
        (xs_ref, b_ref, c_ref, cs_ref, dtv_ref, cst_ref, dtt_ref, tot_ref,
         yb_ref, zr0_ref, zr1_ref, zm0_ref, zm1_ref, dsk_ref, nw_ref, o_ref, om_ref, s_ref) = refs
        z_refs = ((zr0_ref, zm0_ref), (zr1_ref, zm1_ref))
    else:
        (xs_ref, b_ref, c_ref, cs_ref, dtv_ref, cst_ref, dtt_ref, tot_ref, o_ref, s_ref) = refs
    step = pl.program_id(1)

    @pl.when(step == 0)
    def _():
        s_ref[...] = jnp.zeros_like(s_ref)

    ti = _iota((CHUNK, CHUNK), 0)
    si = _iota((CHUNK, CHUNK), 1)
    mask = (si > ti) if rev else (si <= ti)
    lane = _iota((CHUNK, LANE), 1)
    lo_half = lane < SSD_HEAD_DIM
    zero = jnp.zeros((), BF16)

    for grp in range(SSD_GROUPS):
        cols = slice(grp * SSD_GW, (grp + 1) * SSD_GW)
        xs = xs_ref[:, cols]
        bm = b_ref[:, grp * LANE:(grp + 1) * LANE]
        cm = c_ref[:, grp * LANE:(grp + 1) * LANE]
        a = cs_ref[grp]
        dt = dtv_ref[grp]
        a_t = cst_ref[grp]
        dt_t = dtt_ref[grp]
        tot = tot_ref[grp, 0:1, 0:LANE]
        decx = jnp.exp(tot_ref[grp, 0:1, LANE:LANE + SSD_GW])

        g = _nt(cm, bm)
        if rev:
            wst = dt * jnp.exp(a)
        else:
            wst = dt * jnp.exp(tot - a)
        cm32 = cm.astype(F32)
        state = s_ref[grp]
        sb = state.astype(BF16)
        pairs = []
        for p in range(SSD_HPG // 2):
            xp = xs[:, p * LANE:(p + 1) * LANE]
            sp = sb[:, p * LANE:(p + 1) * LANE]
            lhs, rhs = [], []
            for q in range(2):
                h = 2 * p + q
                acol = jnp.broadcast_to(a[:, h:h + 1], (CHUNK, CHUNK))
                arow = a_t[h:h + 1, :]
                e = (arow - acol) if rev else (acol - arow)
                wm = g * jnp.exp(jnp.where(mask, e, NEG)) * dt_t[h:h + 1, :]
                if rev:
                    t0 = LANE + SSD_GW + h * LANE
                    e1 = jnp.exp(tot_ref[grp, 0:1, t0:t0 + LANE] - acol)
                else:
                    e1 = jnp.exp(acol)
                mine = jnp.logical_not(lo_half) if q else lo_half
                lhs += [wm.astype(BF16), (cm32 * e1).astype(BF16)]
                rhs += [jnp.where(mine, xp, zero), jnp.where(mine, sp, zero)]
            pairs.append(_dot(jnp.concatenate(lhs, axis=1), jnp.concatenate(rhs, axis=0)))
        y = jnp.concatenate(pairs, axis=1)
        xw = (xs.astype(F32) * _widen_heads(wst)).astype(BF16)
        s_ref[grp] = decx * state + _tn(bm, xw)

        if final:
            zr_ref, zm_ref = z_refs[grp]
            z = jnp.where(step == 0, _meta_front(zm_ref[...], SSD_GW, BF16), zr_ref[...]).astype(F32)
            yt = y + yb_ref[:, cols] + xs.astype(F32) * dsk_ref[:, cols]
            yt = yt * _silu(z)
            ms = jnp.mean(yt * yt, axis=-1, keepdims=True)
            y = yt * lax.rsqrt(ms + EPS) * nw_ref[:, cols]
        o_ref[:, cols] = y.astype(o_ref.dtype)

    if final:
        @pl.when(step == 0)
        def _():
            om_ref[...] = o_ref[META_PAD:, :]


def ssd_scan(act, cs4, dt4, cst4, dtt4, tot4, grp, *, rev, ybwd=None, pr=None, pm=None,
             d_skip=None, norm_w=None):
    final = not rev
    nc1, bsz = grp.nc1, grp.bsz
    d = 1 if rev else 0
    cc = (lambda c: nc1 - 1 - c) if rev else (lambda c: c)
    bc_w = SSD_GROUPS * SSD_STATE
    b_blk = (COL_XBC + D_INNER) // bc_w
    ng = SSD_GROUPS
    in_specs = [
        pl.BlockSpec((None, CHUNK, D_INNER), lambda b, c: (b, cc(c), COL_XBC // D_INNER)),
        pl.BlockSpec((None, CHUNK, bc_w), lambda b, c: (b, cc(c), b_blk)),
        pl.BlockSpec((None, CHUNK, bc_w), lambda b, c: (b, cc(c), b_blk + 1)),
        pl.BlockSpec((None, ng, CHUNK, LANE), lambda b, c: (b, d, cc(c), 0)),
        pl.BlockSpec((None, ng, CHUNK, LANE), lambda b, c: (b, d, cc(c), 0)),
        pl.BlockSpec((None, ng, None, SSD_HPG, LANE), lambda b, c: (b, d, cc(c), 0, 0)),
        pl.BlockSpec((None, ng, None, SSD_HPG, LANE), lambda b, c: (b, d, cc(c), 0, 0)),
        pl.BlockSpec((None, ng, None, 8, SSD_TOT_W), lambda b, c: (b, d, cc(c), 0, 0)),
    ]
    args = [act, act, act, cs4, dt4, cst4, dtt4, tot4]
    if final:
        r128, mb = grp.rblk(CHUNK), grp.mblk()
        z_blk = COL_Z // SSD_GW

        def z_real(g):
            return pl.BlockSpec((CHUNK, SSD_GW), lambda b, c: (r128(b) + jnp.maximum(c - 1, 0), z_blk + g))

        def z_meta(g):
            return pl.BlockSpec((N_META, SSD_GW), lambda b, c: (mb(b), z_blk + g))

        in_specs += [
            pl.BlockSpec((None, CHUNK, D_INNER), lambda b, c: (b, c, 0)),
            z_real(0), z_real(1), z_meta(0), z_meta(1),
            pl.BlockSpec((1, D_INNER), lambda b, c: (0, 0)),
            pl.BlockSpec((1, D_INNER), lambda b, c: (0, 0)),
        ]
        args += [ybwd, pr, pr, pm, pm, d_skip, norm_w]
        out_specs = [pl.BlockSpec((CHUNK, D_INNER), lambda b, c: (r128(b) + jnp.maximum(c - 1, 0), 0)),
                     pl.BlockSpec((N_META, D_INNER), lambda b, c: (b, 0))]
        out_shape = [jax.ShapeDtypeStruct((bsz * grp.t, D_INNER), BF16),
                     jax.ShapeDtypeStruct((bsz * N_META, D_INNER), BF16)]
    else:
        out_specs = pl.BlockSpec((None, CHUNK, D_INNER), lambda b, c: (b, cc(c), 0))
        out_shape = jax.ShapeDtypeStruct((bsz, grp.lp, D_INNER), F32)
    return pl.pallas_call(
        functools.partial(_ssd_scan_kernel, rev=rev, final=final),
        grid=(bsz, nc1),
        in_specs=in_specs,
        out_specs=out_specs,
        out_shape=out_shape,
        scratch_shapes=[pltpu.VMEM((SSD_GROUPS, SSD_STATE, SSD_GW), F32)],
        compiler_params=_params(("parallel", "arbitrary")),
        name="ssd_scan_" + ("bwd" if rev else "fwd"),
    )(*args)


def _ret_scan_kernel(*refs, rev, final, nc1):
    if final:
        (dl_ref, qr_ref, qm_ref, kr_ref, km_ref, vr_ref, vm_ref, yb_ref, gr_ref, gm_ref,
         o_ref, om_ref, s_ref, dm_ref, e1_ref, ws_ref, dec_ref) = refs
    else:
        (dl_ref, qr_ref, qm_ref, kr_ref, km_ref, vr_ref, vm_ref,
         o_ref, s_ref, dm_ref, e1_ref, ws_ref, dec_ref) = refs
    step = pl.program_id(1)
    chunk = (nc1 - 1 - step) if rev else step
    is_meta = chunk == 0

    @pl.when(step == 0)
    def _():
        s_ref[...] = jnp.zeros_like(s_ref)
        ti = _iota((CHUNK, CHUNK), 0)
        si = _iota((CHUNK, CHUNK), 1)
        tf = ti.astype(F32)
        for h in range(RET_HEADS):
            x = jnp.full((CHUNK, CHUNK), dl_ref[1 if rev else 0, h], F32)
            lg = jnp.minimum(x, 0.0) - jnp.log1p(jnp.exp(-jnp.abs(x)))
            if rev:
                dm_ref[h] = jnp.exp(jnp.where(si > ti, (si - ti).astype(F32) * lg, NEG))
                e1_ref[h] = jnp.exp((CHUNK - tf) * lg)
                ws_ref[h] = jnp.exp(tf * lg)
            else:
                dm_ref[h] = jnp.exp(jnp.where(si <= ti, (ti - si).astype(F32) * lg, NEG))
                e1_ref[h] = jnp.exp((tf + 1.0) * lg)
                ws_ref[h] = jnp.exp((CHUNK - 1.0 - tf) * lg)
            dec_ref[h] = jnp.exp(CHUNK * lg)

    def pick(r_ref, m_ref):
        width = r_ref.shape[-1]
        return jnp.where(is_meta, _meta_front(m_ref[...], width, BF16), r_ref[...])

    q = pick(qr_ref, qm_ref)
    k = pick(kr_ref, km_ref)
    v = pick(vr_ref, vm_ref)
    if final:
        gate = pick(gr_ref, gm_ref)
    lane = _iota((CHUNK, LANE), 1)
    zero = jnp.zeros((), BF16)
    scale = jnp.asarray(RET_K_DIM ** -0.5, BF16)
    heads = range(RET_HEADS)
    pairs = [slice((h // 2) * LANE, (h // 2 + 1) * LANE) for h in heads]
    cols = [slice(h * LANE, (h + 1) * LANE) for h in heads]
    qh = [jnp.where((lane >= RET_K_DIM) if h % 2 else (lane < RET_K_DIM), q[:, pairs[h]], zero) * scale
          for h in heads]
    sc = [_nt(qh[h], k[:, pairs[h]]) for h in heads]
    inter = [_dot(qh[h], s_ref[h].astype(BF16)) for h in heads]
    for h in heads:
        vw = (v[:, cols[h]].astype(F32) * ws_ref[h]).astype(BF16)
        s_ref[h] = dec_ref[h] * s_ref[h] + _tn(k[:, pairs[h]], vw)
    ys = [_dot((sc[h] * dm_ref[h]).astype(BF16), v[:, cols[h]]) + e1_ref[h] * inter[h] for h in heads]
    for h in heads:
        y = ys[h]
        if final:
            yt = y + yb_ref[:, cols[h]]
            mu = jnp.mean(yt, axis=-1, keepdims=True)
            dev = yt - mu
            var = jnp.mean(dev * dev, axis=-1, keepdims=True)
            y = dev * lax.rsqrt(var + 1e-5) * _silu(gate[:, cols[h]].astype(F32))
        o_ref[:, cols[h]] = y.astype(o_ref.dtype)

    if final:
        @pl.when(step == 0)
        def _():
            om_ref[...] = o_ref[META_PAD:, :]


def ret_scan(pr, pm, decay, grp, *, rev, ybwd=None):
    final = not rev
    nc1, bsz = grp.nc1, grp.bsz
    r128, mb = grp.rblk(CHUNK), grp.mblk()
    cc = (lambda c: nc1 - 1 - c) if rev else (lambda c: c)
    qk_w = RET_HEADS * RET_K_DIM

    def real(col0, width):
        blk = col0 // width
        return pl.BlockSpec((CHUNK, width), lambda b, c: (r128(b) + jnp.maximum(cc(c) - 1, 0), blk))

    def meta(col0, width):
        blk = col0 // width
        return pl.BlockSpec((N_META, width), lambda b, c: (mb(b), blk))

    in_specs = [pl.BlockSpec(memory_space=pltpu.SMEM),
                real(COL_RET_Q, qk_w), meta(COL_RET_Q, qk_w),
                real(COL_RET_K, qk_w), meta(COL_RET_K, qk_w),
                real(COL_RET_V, GROUP_WIDTH), meta(COL_RET_V, GROUP_WIDTH)]
    args = [decay, pr, pm, pr, pm, pr, pm]
    if final:
        in_specs += [pl.BlockSpec((None, CHUNK, GROUP_WIDTH), lambda b, c: (b, c, 0)),
                     real(COL_RET_G, GROUP_WIDTH), meta(COL_RET_G, GROUP_WIDTH)]
        args += [ybwd, pr, pm]
        out_specs = [pl.BlockSpec((CHUNK, GROUP_WIDTH), lambda b, c: (r128(b) + jnp.maximum(c - 1, 0), 0)),
                     pl.BlockSpec((N_META, GROUP_WIDTH), lambda b, c: (b, 0))]
        out_shape = [jax.ShapeDtypeStruct((bsz * grp.t, GROUP_WIDTH), BF16),
                     jax.ShapeDtypeStruct((bsz * N_META, GROUP_WIDTH), BF16)]
    else:
        out_specs = pl.BlockSpec((None, CHUNK, GROUP_WIDTH), lambda b, c: (b, cc(c), 0))
        out_shape = jax.ShapeDtypeStruct((bsz, grp.lp, GROUP_WIDTH), F32)
    table = pltpu.VMEM((RET_HEADS, CHUNK, CHUNK), F32)
    return pl.pallas_call(
        functools.partial(_ret_scan_kernel, rev=rev, final=final, nc1=nc1),
        grid=(bsz, nc1),
        in_specs=in_specs,
        out_specs=out_specs,
        out_shape=out_shape,
        scratch_shapes=[pltpu.VMEM((RET_HEADS, LANE, RET_V_DIM), F32), table, table, table, table],
        compiler_params=_params(("parallel", "arbitrary")),
        name="ret_scan_" + ("bwd" if rev else "fwd"),
    )(*args)


def _alibi_slope_rows(rows_per_head, nrows, g):
    hd = _iota((nrows, 1), 0) // rows_per_head
    s = jnp.full((nrows, 1), 2.0 ** -SWA_REP, F32)
    for r in range(SWA_REP - 1):
        s = jnp.where(hd == r, 2.0 ** -(r + 1), s)
    return s * jnp.where(g == 1, 2.0 ** -SWA_REP, 1.0)


def _sink_rows(sink_ref, rows_per_head, nrows, g):
    hd = _iota((nrows, 1), 0) // rows_per_head
    s = jnp.full((nrows, 1), sink_ref[g * SWA_REP + SWA_REP - 1], F32)
    for r in range(SWA_REP - 1):
        s = jnp.where(hd == r, sink_ref[g * SWA_REP + r], s)
    return s


def _softmax_with_sink(s, sink, vall):
    m = jnp.maximum(jnp.max(s, axis=-1, keepdims=True), sink)
    p = jnp.exp(s - m)
    denom = jnp.sum(p, axis=-1, keepdims=True) + jnp.exp(sink - m)
    return _dot(p.astype(BF16), vall) / denom


def _swa_kernel(sink_ref, q0_ref, q1_ref, kp_ref, ko_ref, kn_ref, vp_ref, vo_ref, vn_ref, km_ref, vm_ref,
                o_ref, bias_ref, s_ref, p_ref, l_ref, *, nb):
    q_refs = (q0_ref, q1_ref)
    n = pl.program_id(1)
    blk = SWA_BLOCK
    nq = SWA_REP * blk
    nkeys = 4 * blk
    groups = range(SWA_KV_HEADS)

    @pl.when(n == 0)
    def _():
        t = _iota((nq, nkeys), 0) % blk
        col = _iota((nq, nkeys), 1)
        rel = col - blk - t
        dist = jnp.abs(rel)
        band = jnp.logical_and(col < 3 * blk, dist <= SWA_WINDOW)
        is_meta = jnp.logical_and(col >= 3 * blk, col < 3 * blk + N_META)
        for g in groups:
            slope = _alibi_slope_rows(blk, nq, g)
            bias_ref[g] = jnp.where(band, -slope * dist.astype(F32), jnp.where(is_meta, 0.0, NEG))

    pad = jnp.zeros((META_PAD, SWA_KV_HEADS * SWA_HEAD_DIM), BF16)
    kall = jnp.concatenate([kp_ref[...], ko_ref[...], kn_ref[...], km_ref[...], pad], axis=0)
    vall = jnp.concatenate([vp_ref[...], vo_ref[...], vn_ref[...], vm_ref[...], pad], axis=0)
    for g, q_ref in enumerate(q_refs):
        q = q_ref[...]
        qs = jnp.concatenate([q[:, r * blk:(r + 1) * blk] for r in range(SWA_REP)], axis=0)
        s_ref[g] = _nt(qs, kall[:, g * LANE:(g + 1) * LANE])
    col = _iota((1, nkeys), 1)
    lo = jnp.where(n == 0, blk, 0)
    hi = jnp.where(n == nb - 1, 2 * blk, 3 * blk)
    outside = jnp.logical_or(col < lo, jnp.logical_and(col >= hi, col < 3 * blk))
    scale = SWA_HEAD_DIM ** -0.5
    outs = []
    for g in groups:
        sink = _sink_rows(sink_ref, blk, nq, g)
        for c in range(nq // SWA_SM_ROWS):
            rows = slice(c * SWA_SM_ROWS, (c + 1) * SWA_SM_ROWS)
            s = jnp.where(outside, NEG, s_ref[g, rows, :] * scale + bias_ref[g, rows, :])
            sk = sink[rows]
            m = jnp.maximum(jnp.max(s, axis=-1, keepdims=True), sk)
            p = jnp.exp(s - m)
            p_ref[g, rows, :] = p.astype(BF16)
            denom = jnp.sum(p, axis=-1, keepdims=True) + jnp.exp(sk - m)
            l_ref[g, rows, :] = jnp.broadcast_to(1.0 / denom, (SWA_SM_ROWS, LANE))
        o = _dot(p_ref[g], vall[:, g * LANE:(g + 1) * LANE]) * l_ref[g]
        outs += [o[r * blk:(r + 1) * blk] for r in range(SWA_REP)]
    o_ref[...] = jnp.concatenate(outs, axis=1).astype(o_ref.dtype)


def swa_real(pr, pm, sink, grp):
    bsz, nb = grp.bsz, grp.t // SWA_BLOCK
    r128, mb = grp.rblk(SWA_BLOCK), grp.mblk()
    kvw = SWA_KV_HEADS * SWA_HEAD_DIM
    qw = SWA_REP * SWA_HEAD_DIM
    q_blk, k_blk, v_blk = COL_SWA_Q // qw, COL_SWA_K // kvw, COL_SWA_V // kvw
    nq, nkeys = SWA_REP * SWA_BLOCK, 4 * SWA_BLOCK

    def kv(col_blk, off):
        return pl.BlockSpec((SWA_BLOCK, kvw),
                            lambda b, n: (r128(b) + jnp.clip(n + off, 0, nb - 1), col_blk))

    in_specs = [pl.BlockSpec(memory_space=pltpu.SMEM),
                pl.BlockSpec((SWA_BLOCK, qw), lambda b, n: (r128(b) + n, q_blk)),
                pl.BlockSpec((SWA_BLOCK, qw), lambda b, n: (r128(b) + n, q_blk + 1)),
                kv(k_blk, -1), kv(k_blk, 0), kv(k_blk, 1),
                kv(v_blk, -1), kv(v_blk, 0), kv(v_blk, 1),
                pl.BlockSpec((N_META, kvw), lambda b, n: (mb(b), k_blk)),
                pl.BlockSpec((N_META, kvw), lambda b, n: (mb(b), v_blk))]
    return pl.pallas_call(
        functools.partial(_swa_kernel, nb=nb),
        grid=(bsz, nb),
        in_specs=in_specs,
        out_specs=pl.BlockSpec((SWA_BLOCK, GROUP_WIDTH), lambda b, n: (b * nb + n, 0)),
        out_shape=jax.ShapeDtypeStruct((bsz * grp.t, GROUP_WIDTH), BF16),
        scratch_shapes=[pltpu.VMEM((SWA_KV_HEADS, nq, nkeys), F32),
                        pltpu.VMEM((SWA_KV_HEADS, nq, nkeys), F32),
                        pltpu.VMEM((SWA_KV_HEADS, nq, nkeys), BF16),
                        pltpu.VMEM((SWA_KV_HEADS, nq, LANE), F32)],
        compiler_params=_params(("parallel", "arbitrary")),
        name="swa_real",
    )(sink, pr, pr, pr, pr, pr, pr, pr, pr, pm, pm)


def _swa_meta_kernel(sink_ref, q_ref, k0_ref, v0_ref, km_ref, vm_ref, o_ref):
    g = pl.program_id(1)
    nq = SWA_REP * N_META
    nkeys = 2 * SWA_BLOCK
    q = q_ref[...]
    qs = jnp.concatenate([q[:, r * LANE:(r + 1) * LANE] for r in range(SWA_REP)], axis=0)
    pad = jnp.zeros((META_PAD, SWA_HEAD_DIM), BF16)
    kall = jnp.concatenate([k0_ref[...], km_ref[...], pad], axis=0)
    vall = jnp.concatenate([v0_ref[...], vm_ref[...], pad], axis=0)
    i = _iota((nq, nkeys), 0) % N_META
    col = _iota((nq, nkeys), 1)
    mdist = N_META + col - i
    near = jnp.logical_and(col < SWA_BLOCK, mdist <= SWA_WINDOW)
    is_meta = jnp.logical_and(col >= SWA_BLOCK, col < SWA_BLOCK + N_META)
    slope = _alibi_slope_rows(N_META, nq, g)
    bias = jnp.where(near, -slope * mdist.astype(F32), jnp.where(is_meta, 0.0, NEG))
    s = _nt(qs, kall) * (SWA_HEAD_DIM ** -0.5) + bias
    o = _softmax_with_sink(s, _sink_rows(sink_ref, N_META, nq, g), vall)
    o_ref[...] = jnp.concatenate([o[r * N_META:(r + 1) * N_META] for r in range(SWA_REP)],
                                 axis=1).astype(o_ref.dtype)


def swa_meta(pr, pm, sink, grp):
    bsz = grp.bsz
    r128, mb = grp.rblk(SWA_BLOCK), grp.mblk()
    qw = SWA_REP * SWA_HEAD_DIM
    q_blk, k_blk, v_blk = COL_SWA_Q // qw, COL_SWA_K // LANE, COL_SWA_V // LANE
    in_specs = [pl.BlockSpec(memory_space=pltpu.SMEM),
                pl.BlockSpec((N_META, qw), lambda b, g: (mb(b), q_blk + g)),
                pl.BlockSpec((SWA_BLOCK, LANE), lambda b, g: (r128(b), k_blk + g)),
                pl.BlockSpec((SWA_BLOCK, LANE), lambda b, g: (r128(b), v_blk + g)),
                pl.BlockSpec((N_META, LANE), lambda b, g: (mb(b), k_blk + g)),
                pl.BlockSpec((N_META, LANE), lambda b, g: (mb(b), v_blk + g))]
    return pl.pallas_call(
        _swa_meta_kernel,
        grid=(bsz, SWA_KV_HEADS),
        in_specs=in_specs,
        out_specs=pl.BlockSpec((N_META, qw), lambda b, g: (b, g)),
        out_shape=jax.ShapeDtypeStruct((bsz * N_META, GROUP_WIDTH), BF16),
        compiler_params=_params(("parallel", "parallel")),
        name="swa_meta",
    )(sink, pm, pr, pr, pm, pm)


NA_KROWS = 3 * NA_QROWS
NA_RPB_H = 2 * NA_KH - 1
NA_RPB_W = 2 * NA_KW - 1
NA_VARIANTS = 3
NA_SM_ROWS = 32
NA_HPS = 8


def _na_row_ok(variant, i, j):
    if variant == 0:
        return NA_QROWS <= j < NA_QROWS + NA_KH
    if variant == 1:
        return i <= j < i + NA_KH
    return j < NA_KH


def _na_bias_kernel(rpb_ref, o_ref):
    h = pl.program_id(0)
    base = h * (NA_RPB_H * NA_RPB_W)
    qc = _iota((GRID_W, LANE), 0)
    lane = _iota((GRID_W, LANE), 1)
    lo_half = lane < GRID_W
    kc = jnp.where(lo_half, lane, lane - GRID_W)
    d = kc - qc + (NA_KW - 1)
    col_start = jnp.clip(qc - NA_KW // 2, 0, GRID_W - NA_KW)
    col_ok = jnp.logical_and(kc >= col_start, kc < col_start + NA_KW)
    neg = jnp.full((GRID_W, LANE), NEG, F32)
    for a in range(NA_RPB_H - 1):
        val = jnp.zeros((GRID_W, LANE), F32)
        for b in range(NA_RPB_W):
            s_lo = rpb_ref[base + a * NA_RPB_W + b]
            s_hi = rpb_ref[base + (a + 1) * NA_RPB_W + b]
            val = jnp.where(d == b, jnp.where(lo_half, s_lo, s_hi), val)
        tile = jnp.where(col_ok, val, neg)
        for jp in range(NA_KROWS // 2):
            i = 2 * jp + (NA_QROWS - 1) - a
            if not 0 <= i < NA_QROWS:
                continue
            for v in range(NA_VARIANTS):
                ok_lo, ok_hi = _na_row_ok(v, i, 2 * jp), _na_row_ok(v, i, 2 * jp + 1)
                if ok_lo and ok_hi:
                    blk = tile
                elif ok_lo:
                    blk = jnp.where(lo_half, tile, neg)
                elif ok_hi:
                    blk = jnp.where(lo_half, neg, tile)
                else:
                    blk = neg
                o_ref[v, i * GRID_W:(i + 1) * GRID_W, jp * LANE:(jp + 1) * LANE] = blk
    meta_cols = jnp.where(_iota((NA_QB, LANE), 1) < N_META, 0.0, NEG)
    for v in range(NA_VARIANTS):
        o_ref[v, :, 3 * NA_QB:] = meta_cols


def na_bias(rpb):
    return pl.pallas_call(
        _na_bias_kernel,
        grid=(NA_HEADS,),
        in_specs=[pl.BlockSpec(memory_space=pltpu.SMEM)],
        out_specs=pl.BlockSpec((NA_VARIANTS, None, NA_QB, NA_KEYS), lambda h: (0, h, 0, 0)),
        out_shape=jax.ShapeDtypeStruct((NA_VARIANTS, NA_HEADS, NA_QB, NA_KEYS), F32),
        compiler_params=_params(("parallel",)),
        name="na_bias",
    )(rpb.astype(F32).reshape(-1))


def _na_kernel(q_ref, kp_ref, ko_ref, kn_ref, vp_ref, vo_ref, vn_ref, km_ref, vm_ref,
               bias_ref, o_ref, s_ref, p_ref, l_ref):
    q = q_ref[...] * jnp.asarray(NA_HEAD_DIM ** -0.5, BF16)
    width = NA_HPS * NA_HEAD_DIM
    pad = jnp.zeros((META_PAD, width), BF16)
    kall = jnp.concatenate([kp_ref[...], ko_ref[...], kn_ref[...], km_ref[...], pad], axis=0)
    vall = jnp.concatenate([vp_ref[...], vo_ref[...], vn_ref[...], vm_ref[...], pad], axis=0)
    zero = jnp.zeros((), BF16)
    qlane = _iota((NA_QB, LANE), 1)
    vlane = _iota((NA_KEYS, LANE), 1)
    tiles = [slice((hh // 2) * LANE, (hh // 2 + 1) * LANE) for hh in range(NA_HPS)]
    for hh in range(NA_HPS):
        qsel = (qlane >= NA_HEAD_DIM) if hh % 2 else (qlane < NA_HEAD_DIM)
        s_ref[hh] = _nt(jnp.where(qsel, q[:, tiles[hh]], zero), kall[:, tiles[hh]])
    outs = [None] * (NA_HPS // 2)
    for hh in range(NA_HPS):
        for c in range(NA_QB // NA_SM_ROWS):
            rows = slice(c * NA_SM_ROWS, (c + 1) * NA_SM_ROWS)
            s = s_ref[hh, rows, :] + bias_ref[hh, rows, :]
            m = jnp.max(s, axis=-1, keepdims=True)
            p = jnp.exp(s - m)
            p_ref[hh, rows, :] = p.astype(BF16)
            l_ref[hh, rows, :] = jnp.broadcast_to(1.0 / jnp.sum(p, axis=-1, keepdims=True),
                                                  (NA_SM_ROWS, LANE))
        vsel = (vlane >= NA_HEAD_DIM) if hh % 2 else (vlane < NA_HEAD_DIM)
        o = _dot(p_ref[hh], jnp.where(vsel, vall[:, tiles[hh]], zero)) * l_ref[hh]
        outs[hh // 2] = o if outs[hh // 2] is None else outs[hh // 2] + o
    o_ref[...] = jnp.concatenate(outs, axis=1).astype(o_ref.dtype)


def na_real(pr, pm, bias, grp):
    bsz, nqb = grp.bsz, grp.t // NA_QB
    rq, mb = grp.rblk(NA_QB), grp.mblk()
    width = NA_HPS * NA_HEAD_DIM
    q_blk, k_blk, v_blk = COL_NA_Q // width, COL_NA_K // width, COL_NA_V // width

    def kv(col_blk, off):
        return pl.BlockSpec((NA_QB, width),
                            lambda p, b, n: (rq(b) + jnp.clip(n + off, 0, nqb - 1), col_blk + p))

    in_specs = [pl.BlockSpec((NA_QB, width), lambda p, b, n: (rq(b) + n, q_blk + p)),
                kv(k_blk, -1), kv(k_blk, 0), kv(k_blk, 1),
                kv(v_blk, -1), kv(v_blk, 0), kv(v_blk, 1),
                pl.BlockSpec((N_META, width), lambda p, b, n: (mb(b), k_blk + p)),
                pl.BlockSpec((N_META, width), lambda p, b, n: (mb(b), v_blk + p)),
                pl.BlockSpec((None, NA_HPS, NA_QB, NA_KEYS),
                             lambda p, b, n: (jnp.where(n == 0, 0, jnp.where(n == nqb - 1, 2, 1)), p, 0, 0))]
    return pl.pallas_call(
        _na_kernel,
        grid=(NA_HEADS // NA_HPS, bsz, nqb),
        in_specs=in_specs,
        out_specs=pl.BlockSpec((NA_QB, width), lambda p, b, n: (b * nqb + n, p)),
        out_shape=jax.ShapeDtypeStruct((bsz * grp.t, GROUP_WIDTH), BF16),
        scratch_shapes=[pltpu.VMEM((NA_HPS, NA_QB, NA_KEYS), F32),
                        pltpu.VMEM((NA_HPS, NA_QB, NA_KEYS), BF16),
                        pltpu.VMEM((NA_HPS, NA_QB, LANE), F32)],
        compiler_params=_params(("parallel", "parallel", "arbitrary")),
        name="na_real",
    )(pr, pr, pr, pr, pr, pr, pr, pm, pm, bias)


def _na_meta_kernel(q_ref, k_ref, v_ref, km_ref, vm_ref, o_ref):
    nwin = NA_KH * GRID_W
    nkeys = nwin + CHUNK
    q = q_ref[...]
    pad = jnp.zeros((META_PAD, LANE), BF16)
    kall = jnp.concatenate([k_ref[...], km_ref[...], pad], axis=0)
    vall = jnp.concatenate([v_ref[...], vm_ref[...], pad], axis=0)
    col = _iota((N_META, nkeys), 1)
    in_win = jnp.logical_and(col < nwin, col % GRID_W < NA_KW)
    is_meta = jnp.logical_and(col >= nwin, col < nwin + N_META)
    bias = jnp.where(jnp.logical_or(in_win, is_meta), 0.0, NEG)
    zero = jnp.zeros((), BF16)
    qlane = _iota((N_META, LANE), 1)
    vlane = _iota((nkeys, LANE), 1)
    out = jnp.zeros((N_META, LANE), F32)
    for hh in range(2):
        qsel = (qlane >= NA_HEAD_DIM) if hh else (qlane < NA_HEAD_DIM)
        vsel = (vlane >= NA_HEAD_DIM) if hh else (vlane < NA_HEAD_DIM)
        s = _nt(jnp.where(qsel, q, zero), kall) * (NA_HEAD_DIM ** -0.5) + bias
        m = jnp.max(s, axis=-1, keepdims=True)
        p = jnp.exp(s - m)
        denom = jnp.sum(p, axis=-1, keepdims=True)
        out = out + _dot(p.astype(BF16), jnp.where(vsel, vall, zero)) / denom
    o_ref[...] = out.astype(o_ref.dtype)


def na_meta(pr, pm, grp):
    bsz = grp.bsz
    nwin = NA_KH * GRID_W
    rw, mb = grp.rblk(nwin), grp.mblk()
    q_blk, k_blk, v_blk = COL_NA_Q // LANE, COL_NA_K // LANE, COL_NA_V // LANE
    in_specs = [pl.BlockSpec((N_META, LANE), lambda b, p: (mb(b), q_blk + p)),
                pl.BlockSpec((nwin, LANE), lambda b, p: (rw(b), k_blk + p)),
                pl.BlockSpec((nwin, LANE), lambda b, p: (rw(b), v_blk + p)),
                pl.BlockSpec((N_META, LANE), lambda b, p: (mb(b), k_blk + p)),
                pl.BlockSpec((N_META, LANE), lambda b, p: (mb(b), v_blk + p))]
    return pl.pallas_call(
        _na_meta_kernel,
        grid=(bsz, NA_HEADS // 2),
        in_specs=in_specs,
        out_specs=pl.BlockSpec((N_META, LANE), lambda b, p: (b, p)),
        out_shape=jax.ShapeDtypeStruct((bsz * N_META, GROUP_WIDTH), BF16),
        compiler_params=_params(("parallel", "parallel")),
        name="na_meta",
    )(pm, pr, pr, pm, pm)


def mix_group(pr, pm, dtr, dtm, lw, grp, need_meta):
    act, cs4, dt4, cst4, dtt4, tot4 = ssd_pre(pr, pm, dtr, dtm, lw["conv_w"], lw["conv_b"],
                                               lw["bias4"], lw["alog4"], grp)
    yb = ssd_scan(act, cs4, dt4, cst4, dtt4, tot4, grp, rev=True)
    y_ssd, ym_ssd = ssd_scan(act, cs4, dt4, cst4, dtt4, tot4, grp, rev=False, ybwd=yb, pr=pr, pm=pm,
                             d_skip=lw["d_skip"], norm_w=lw["ssd_norm_w"])
    rb = ret_scan(pr, pm, lw["ret_decay"], grp, rev=True)
    y_ret, ym_ret = ret_scan(pr, pm, lw["ret_decay"], grp, rev=False, ybwd=rb)
    real = [y_ssd, swa_real(pr, pm, lw["sink"], grp), na_real(pr, pm, lw["na_bias"], grp), y_ret]
    meta = None
    if need_meta:
        meta = [ym_ssd, swa_meta(pr, pm, lw["sink"], grp), na_meta(pr, pm, grp), ym_ret]
    return real, meta


def _layer_weights(i, w_in, ssd_conv_w, ssd_conv_b, ssd_dt_bias, ssd_a_log, ssd_d, ssd_norm_w,
                   swa_sink, na_rpb, ret_decay, w_out, w_up, w_down):
    w_main, w_dt = cast_w_in(w_in, i)

    def per_dir_group(v):
        v4 = v.astype(F32).reshape(2 * SSD_GROUPS, 1, SSD_HPG)
        return jnp.pad(v4, ((0, 0), (0, 0), (0, LANE - SSD_HPG)))

    return dict(
        w_main=w_main, w_dt=w_dt,
        conv_w=jnp.pad(ssd_conv_w[i].astype(F32), ((0, 8 - SSD_CONV_W), (0, 0))),
        conv_b=ssd_conv_b[i].astype(F32).reshape(1, SSD_CONV_DIM),
        bias4=per_dir_group(ssd_dt_bias[i]), alog4=per_dir_group(ssd_a_log[i]),
        d_skip=jnp.repeat(ssd_d[i].astype(F32), SSD_HEAD_DIM).reshape(1, D_INNER),
        ssd_norm_w=ssd_norm_w[i].astype(F32).reshape(1, D_INNER),
        sink=swa_sink[i].astype(F32), na_bias=na_bias(na_rpb[i]),
        ret_decay=ret_decay[i].astype(F32),
        w_out=cast_bf16(w_out, i), w_up=cast_bf16(w_up, i), w_down=cast_bf16(w_down, i),
    )


TM_REAL = 1024
TN = 1024
TN_BF16 = 1024
TK_DOWN = 4096
TM_NORM = 256


def _mlp(x, tm, lw, norm_w):
    n2 = rmsnorm(x, norm_w, BF16, min(tm, TM_NORM))
    u = matmul(n2, lw["w_up"], tm=tm, tn=TN_BF16, tk=D_MODEL, out_dtype=BF16, epilogue="relu2")
    return matmul(u, lw["w_down"], tm=tm, tn=TN, tk=TK_DOWN, out_dtype=F32,
                  epilogue="residual", residual=x)


def kernel(x_prompt, x_sample, meta_tokens, norm1_w, w_in, ssd_conv_w, ssd_conv_b, ssd_dt_bias,
           ssd_a_log, ssd_d, ssd_norm_w, swa_sink, na_rpb, ret_decay, w_out, norm2_w, w_up, w_down,
           final_norm_w):
    d = D_MODEL
    inputs = [x_prompt, x_sample]
    groups, mbase = [], 0
    for x in inputs:
        groups.append(Group(mbase, x.shape[0], x.shape[1]))
        mbase += x.shape[0] * N_META
    n_meta = mbase
    xs = [x.reshape(-1, d).astype(F32) for x in inputs]
    xm = jnp.tile(meta_tokens.astype(F32), (n_meta // N_META, 1))

    for i in range(DEPTH):
        last = i == DEPTH - 1
        lw = _layer_weights(i, w_in, ssd_conv_w, ssd_conv_b, ssd_dt_bias, ssd_a_log, ssd_d,
                            ssd_norm_w, swa_sink, na_rpb, ret_decay, w_out, w_up, w_down)
        nm = rmsnorm(xm, norm1_w[i], BF16, n_meta)
        pm = matmul(nm, lw["w_main"], tm=n_meta, tn=TN_BF16, tk=d, out_dtype=BF16)
        dtm = matmul(nm, lw["w_dt"], tm=n_meta, tn=LANE, tk=d, out_dtype=F32)
        meta_parts = []
        for gi, grp in enumerate(groups):
            nr = rmsnorm(xs[gi], norm1_w[i], BF16, TM_NORM)
            pr = matmul(nr, lw["w_main"], tm=TM_REAL, tn=TN_BF16, tk=d, out_dtype=BF16)
            dtr = matmul(nr, lw["w_dt"], tm=TM_REAL, tn=LANE, tk=d, out_dtype=F32)
            real, meta = mix_group(pr, pm, dtr, dtm, lw, grp, need_meta=not last)
            meta_parts.append(meta)
            h = matmul(real, lw["w_out"], tm=TM_REAL, tn=TN, tk=GROUP_WIDTH, out_dtype=F32,
                       epilogue="residual", residual=xs[gi], alias=i > 0)
            xs[gi] = _mlp(h, TM_REAL, lw, norm2_w[i])
        if not last:
            mixed_m = [jnp.concatenate([m[k] for m in meta_parts], axis=0) for k in range(4)]
            hm = matmul(mixed_m, lw["w_out"], tm=n_meta, tn=TN, tk=GROUP_WIDTH, out_dtype=F32,
                        epilogue="residual", residual=xm)
            xm = _mlp(hm, n_meta, lw, norm2_w[i])

    outs = [rmsnorm(x, final_norm_w, F32, TM_NORM).reshape(inp.shape)
            for x, inp in zip(xs, inputs)]
    return tuple(outs)
```

```python
import functools
import numpy as np
import jax
import jax.numpy as jnp
from jax import lax
from jax.experimental import pallas as pl
from jax.experimental.pallas import tpu as pltpu

F32 = jnp.float32
BF16 = jnp.bfloat16
HIGHEST = lax.Precision.HIGHEST

D_MODEL = 4096
DEPTH = 2
N_META = 16
GRID_W = 64
GROUP_WIDTH = D_MODEL // 4
D_FF = 4 * D_MODEL
EPS = 1e-6
CHUNK = 128
META_PAD = CHUNK - N_META

SSD_HEAD_DIM = 64
SSD_HEADS = 16
SSD_GROUPS = 2
SSD_HPG = 8
SSD_STATE = 128
SSD_CONV_W = 5
D_INNER = GROUP_WIDTH
SSD_CONV_DIM = D_INNER + 2 * SSD_GROUPS * SSD_STATE
SSD_GW = SSD_HPG * SSD_HEAD_DIM
SSD_CONV_TILE = 256
SSD_TOT_W = 128 + SSD_GW + SSD_HPG * 128

SWA_HEAD_DIM = 128
SWA_HEADS = 8
SWA_KV_HEADS = 2
SWA_REP = SWA_HEADS // SWA_KV_HEADS
SWA_WINDOW = 128
SWA_BLOCK = 128
SWA_SM_ROWS = 32

NA_HEAD_DIM = 64
NA_HEADS = 16
NA_KH = 8
NA_KW = 16
NA_QROWS = 4
NA_QB = NA_QROWS * GRID_W
NA_KEYS = 3 * NA_QB + CHUNK

RET_HEADS = 8
RET_K_DIM = 64
RET_V_DIM = 128

LANE = 128
COL_XBC = 0
COL_Z = 1536
COL_SWA_Q = 2560
COL_SWA_K = 3584
COL_SWA_V = 3840
COL_NA_Q = 4096
COL_NA_K = 5120
COL_NA_V = 6144
COL_RET_Q = 7168
COL_RET_K = 7680
COL_RET_V = 8192
COL_RET_G = 9216
P_COLS = 10240

MXU_WIDTH = 256
NEG = -1e30
VMEM_LIMIT_BYTES = 56 * 1024 * 1024


def _params(sem):
    return pltpu.CompilerParams(dimension_semantics=sem, vmem_limit_bytes=VMEM_LIMIT_BYTES)


def _nt(a, b):
    return lax.dot_general(a, b, (((1,), (1,)), ((), ())), preferred_element_type=F32)


def _tn(a, b):
    return lax.dot_general(a, b, (((0,), (0,)), ((), ())), preferred_element_type=F32)


def _dot(a, b):
    return jnp.dot(a, b, preferred_element_type=F32)


def _dot_exact(a, b):
    return jnp.dot(a, b, preferred_element_type=F32, precision=HIGHEST)


def _silu(x):
    return x * jax.nn.sigmoid(x)


def _softplus(x):
    return jnp.maximum(x, 0.0) + jnp.log1p(jnp.exp(-jnp.abs(x)))


def _iota(shape, dim):
    return lax.broadcasted_iota(jnp.int32, shape, dim)


def _rmsnorm_kernel(x_ref, w_ref, o_ref):
    x = x_ref[...]
    ms = jnp.mean(x * x, axis=-1, keepdims=True)
    o_ref[...] = (x * lax.rsqrt(ms + EPS) * w_ref[...]).astype(o_ref.dtype)


def rmsnorm(x, w, out_dtype, tm):
    m, d = x.shape
    return pl.pallas_call(
        _rmsnorm_kernel,
        grid=(m // tm,),
        in_specs=[pl.BlockSpec((tm, d), lambda i: (i, 0)),
                  pl.BlockSpec((1, d), lambda i: (0, 0))],
        out_specs=pl.BlockSpec((tm, d), lambda i: (i, 0)),
        out_shape=jax.ShapeDtypeStruct((m, d), out_dtype),
        compiler_params=_params(("parallel",)),
        name="rmsnorm",
    )(x, w.reshape(1, d).astype(F32))


CAST_BLOCK_BYTES = 8 * 1024 * 1024


def _cast_kernel(x_ref, o_ref):
    o_ref[...] = x_ref[...].astype(o_ref.dtype)


def cast_bf16(w, layer):
    _, r, c = w.shape
    rows = min(r, CAST_BLOCK_BYTES // (4 * c))
    assert r % rows == 0 and rows % 16 == 0
    return pl.pallas_call(
        _cast_kernel,
        grid=(r // rows,),
        in_specs=[pl.BlockSpec((None, rows, c), lambda i: (layer, i, 0))],
        out_specs=pl.BlockSpec((rows, c), lambda i: (i, 0)),
        out_shape=jax.ShapeDtypeStruct((r, c), BF16),
        compiler_params=_params(("parallel",)),
        name="cast_bf16",
    )(w.astype(F32))


def _cast_w_in_kernel(x_ref, main_ref, dt_ref):
    x0, dt0 = D_INNER, D_INNER + SSD_CONV_DIM
    rest0 = dt0 + 2 * SSD_HEADS
    main_ref[:, COL_XBC:COL_Z] = x_ref[:, x0:dt0].astype(BF16)
    main_ref[:, COL_Z:COL_SWA_Q] = x_ref[:, 0:x0].astype(BF16)
    main_ref[:, COL_SWA_Q:] = x_ref[:, rest0:].astype(BF16)
    lane = _iota((x_ref.shape[0], LANE), 1)
    dt_ref[...] = jnp.where(lane < 2 * SSD_HEADS, x_ref[:, dt0:dt0 + LANE], 0.0).astype(BF16)


def cast_w_in(w_in, layer):
    _, r, c = w_in.shape
    rows = 128
    return pl.pallas_call(
        _cast_w_in_kernel,
        grid=(r // rows,),
        in_specs=[pl.BlockSpec((None, rows, c), lambda i: (layer, i, 0))],
        out_specs=[pl.BlockSpec((rows, P_COLS), lambda i: (i, 0)),
                   pl.BlockSpec((rows, LANE), lambda i: (i, 0))],
        out_shape=[jax.ShapeDtypeStruct((r, P_COLS), BF16), jax.ShapeDtypeStruct((r, LANE), BF16)],
        compiler_params=_params(("parallel",)),
        name="cast_w_in",
    )(w_in.astype(F32))


def _mm_kernel(*refs, n_in, epilogue, nk):
    x_refs, w_refs, rest = refs[:n_in], refs[n_in:2 * n_in], refs[2 * n_in:]
    if epilogue == "residual":
        r_ref, o_ref = rest
    else:
        (o_ref,) = rest
    if nk > 1:
        @pl.when(pl.program_id(2) == 0)
        def _():
            o_ref[...] = r_ref[...]

    tn = o_ref.shape[1]
    for c0 in range(0, tn, MXU_WIDTH):
        cols = slice(c0, min(c0 + MXU_WIDTH, tn))
        part = _dot(x_refs[0][...], w_refs[0][:, cols])
        for x_ref, w_ref in zip(x_refs[1:], w_refs[1:]):
            part = part + _dot(x_ref[...], w_ref[:, cols])
        if nk > 1:
            o_ref[:, cols] += part
        else:
            if epilogue == "relu2":
                part = jnp.square(jnp.maximum(part, 0.0))
            elif epilogue == "residual":
                part = part + r_ref[:, cols]
            o_ref[:, cols] = part.astype(o_ref.dtype)


def matmul(xs, w, *, tm, tn, tk, out_dtype, epilogue="none", residual=None, alias=True):
    xs = list(xs) if isinstance(xs, (list, tuple)) else [xs]
    n_in = len(xs)
    m, kc = xs[0].shape
    n = w.shape[1]
    if n_in > 1:
        assert tk == kc and w.shape[0] == n_in * kc
        nk = 1
        w_specs = [pl.BlockSpec((kc, tn), functools.partial(lambda i, j, k, c: (c, j), c=c))
                   for c in range(n_in)]
    else:
        nk = kc // tk
        w_specs = [pl.BlockSpec((tk, tn), lambda i, j, k: (k, j))]
    assert nk == 1 or (epilogue == "residual" and out_dtype == F32)
    in_specs = [pl.BlockSpec((tm, tk), lambda i, j, k: (i, k)) for _ in xs] + w_specs
    args = xs + [w] * n_in
    aliases = {}
    if epilogue == "residual":
        in_specs.append(pl.BlockSpec((tm, tn), lambda i, j, k: (i, j)))
        args.append(residual)
        if alias:
            aliases = {2 * n_in: 0}
    return pl.pallas_call(
        functools.partial(_mm_kernel, n_in=n_in, epilogue=epilogue, nk=nk),
        grid=(m // tm, n // tn, nk),
        in_specs=in_specs,
        out_specs=pl.BlockSpec((tm, tn), lambda i, j, k: (i, j)),
        out_shape=jax.ShapeDtypeStruct((m, n), out_dtype),
        input_output_aliases=aliases,
        compiler_params=_params(("parallel", "parallel", "arbitrary")),
        name="matmul_" + epilogue,
    )(*args)


class Group:
    def __init__(self, mbase, bsz, t):
        self.mbase = mbase
        self.bsz = bsz
        self.t = t
        self.nc = t // CHUNK
        self.nc1 = self.nc + 1
        self.lp = self.nc1 * CHUNK

    def rblk(self, rows):
        per = self.t // rows
        return lambda b: b * per

    def mblk(self):
        first = self.mbase // N_META
        return lambda b: first + b


def _meta_front(m_val, width, dtype):
    return jnp.concatenate([jnp.zeros((META_PAD, width), dtype), m_val], axis=0)


def _ssd_pre_kernel(cur_ref, prev_ref, next_ref, meta_ref, dtr_ref, dtm_ref, cw_ref, cb_ref,
                    bias4_ref, alog4_ref,
                    act_ref, cs_ref, dtv_ref, cst_ref, dtt_ref, tot_ref, ext_ref, *, nc):
    c = pl.program_id(1)
    is_meta = c == 0
    half = (SSD_CONV_W - 1) // 2
    row = _iota((CHUNK, 1), 0)
    valid = jnp.logical_or(c > 0, row >= META_PAD)
    ext_ref[0:N_META, :] = prev_ref[...]
    ext_ref[N_META:N_META + CHUNK, :] = cur_ref[...]
    ext_ref[N_META + CHUNK:, :] = next_ref[...]

    @pl.when(is_meta)
    def _():
        ext_ref[0:N_META + META_PAD, :] = jnp.zeros((N_META + META_PAD, SSD_CONV_DIM), BF16)
        ext_ref[N_META + META_PAD:N_META + CHUNK, :] = meta_ref[...]

    @pl.when(c == 1)
    def _():
        ext_ref[0:N_META, :] = meta_ref[...]

    @pl.when(c == nc)
    def _():
        ext_ref[N_META + CHUNK:, :] = jnp.zeros((N_META, SSD_CONV_DIM), BF16)

    ext_rows = CHUNK + 2 * N_META
    out_row = _iota((SSD_CONV_W * CHUNK, ext_rows), 0)
    src_row = _iota((SSD_CONV_W * CHUNK, ext_rows), 1)
    shift = (src_row == out_row % CHUNK + out_row // CHUNK + (N_META - half)).astype(BF16)
    for c0 in range(0, SSD_CONV_DIM, SSD_CONV_TILE):
        cols = slice(c0, c0 + SSD_CONV_TILE)
        taps = _dot(shift, ext_ref[:, cols])
        acc = jnp.broadcast_to(cb_ref[:, cols], (CHUNK, SSD_CONV_TILE))
        for j in range(SSD_CONV_W):
            acc = acc + cw_ref[j:j + 1, cols] * taps[j * CHUNK:(j + 1) * CHUNK]
        act_ref[:, cols] = jnp.where(valid, _silu(acc), 0.0).astype(act_ref.dtype)

    dtx = jnp.where(is_meta, _meta_front(dtm_ref[...], LANE, F32), dtr_ref[...])
    lane = _iota((CHUNK, LANE), 1)
    keep = jnp.logical_and(valid, lane < SSD_HPG)
    ti = _iota((CHUNK, CHUNK), 0)
    si = _iota((CHUNK, CHUNK), 1)
    tri = (si <= ti).astype(F32)
    for k in range(2 * SSD_GROUPS):
        x = dtx if k == 0 else pltpu.roll(dtx, LANE - SSD_HPG * k, axis=1)
        dt = jnp.where(keep, _softplus(x + bias4_ref[k]), 0.0)
        la = dt * (-jnp.exp(alog4_ref[k]))
        incl = _dot_exact(tri, la)
        cs = incl if k < SSD_GROUPS else incl - la
        cs_ref[k] = cs
        dtv_ref[k] = dt
        cst_ref[k] = jnp.transpose(cs)[0:SSD_HPG, :]
        dtt_ref[k] = jnp.transpose(dt)[0:SSD_HPG, :]
        last = jnp.broadcast_to(incl[CHUNK - 1:CHUNK, :], (8, LANE))
        tiles = [jnp.broadcast_to(last[:, h:h + 1], (8, LANE)) for h in range(SSD_HPG)]
        halves = [jnp.where(lane[0:8] < SSD_HEAD_DIM, tiles[2 * p], tiles[2 * p + 1])
                  for p in range(SSD_HPG // 2)]
        tot_ref[k] = jnp.concatenate([last] + halves + tiles, axis=1)


def ssd_pre(pr, pm, dtr, dtm, conv_w, conv_b, bias4, alog4, grp):
    nc, nc1, bsz = grp.nc, grp.nc1, grp.bsz
    w = SSD_CONV_DIM
    r128, r16, mb = grp.rblk(CHUNK), grp.rblk(N_META), grp.mblk()
    per16 = CHUNK // N_META
    ng = 2 * SSD_GROUPS
    in_specs = [
        pl.BlockSpec((CHUNK, w), lambda b, c: (r128(b) + jnp.maximum(c - 1, 0), 0)),
        pl.BlockSpec((N_META, w), lambda b, c: (r16(b) + jnp.maximum((c - 1) * per16 - 1, 0), 0)),
        pl.BlockSpec((N_META, w), lambda b, c: (r16(b) + jnp.minimum(c * per16, nc * per16 - 1), 0)),
        pl.BlockSpec((N_META, w), lambda b, c: (mb(b), 0)),
        pl.BlockSpec((CHUNK, LANE), lambda b, c: (r128(b) + jnp.maximum(c - 1, 0), 0)),
        pl.BlockSpec((N_META, LANE), lambda b, c: (mb(b), 0)),
        pl.BlockSpec((8, w), lambda b, c: (0, 0)),
        pl.BlockSpec((1, w), lambda b, c: (0, 0)),
        pl.BlockSpec((ng, 1, LANE), lambda b, c: (0, 0, 0)),
        pl.BlockSpec((ng, 1, LANE), lambda b, c: (0, 0, 0)),
    ]
    out_specs = [
        pl.BlockSpec((None, CHUNK, w), lambda b, c: (b, c, 0)),
        pl.BlockSpec((None, ng, CHUNK, LANE), lambda b, c: (b, 0, c, 0)),
        pl.BlockSpec((None, ng, CHUNK, LANE), lambda b, c: (b, 0, c, 0)),
        pl.BlockSpec((None, ng, None, SSD_HPG, LANE), lambda b, c: (b, 0, c, 0, 0)),
        pl.BlockSpec((None, ng, None, SSD_HPG, LANE), lambda b, c: (b, 0, c, 0, 0)),
        pl.BlockSpec((None, ng, None, 8, SSD_TOT_W), lambda b, c: (b, 0, c, 0, 0)),
    ]
    out_shape = [
        jax.ShapeDtypeStruct((bsz, grp.lp, w), BF16),
        jax.ShapeDtypeStruct((bsz, ng, grp.lp, LANE), F32),
        jax.ShapeDtypeStruct((bsz, ng, grp.lp, LANE), F32),
        jax.ShapeDtypeStruct((bsz, ng, nc1, SSD_HPG, LANE), F32),
        jax.ShapeDtypeStruct((bsz, ng, nc1, SSD_HPG, LANE), F32),
        jax.ShapeDtypeStruct((bsz, ng, nc1, 8, SSD_TOT_W), F32),
    ]
    return pl.pallas_call(
        functools.partial(_ssd_pre_kernel, nc=nc),
        grid=(bsz, nc1),
        in_specs=in_specs, out_specs=out_specs, out_shape=out_shape,
        scratch_shapes=[pltpu.VMEM((CHUNK + 2 * N_META, w), BF16)],
        compiler_params=_params(("parallel", "arbitrary")),
        name="ssd_pre",
    )(pr, pr, pr, pm, dtr, dtm, conv_w, conv_b, bias4, alog4)


def _widen_heads(x):
    sel = (_iota((LANE, SSD_GW), 0) == _iota((LANE, SSD_GW), 1) // SSD_HEAD_DIM).astype(BF16)
    hi = x.astype(BF16)
    r1 = x - hi.astype(F32)
    mid = r1.astype(BF16)
    lo = (r1 - mid.astype(F32)).astype(BF16)
    return _dot(jnp.concatenate([hi, mid, lo], axis=1), jnp.concatenate([sel, sel, sel], axis=0))


def _ssd_scan_kernel(*refs, rev, final):
    if final:
        (xs_ref, b_ref, c_ref, cs_ref, dtv_ref, cst_ref, dtt_ref, tot_ref,
         yb_ref, zr0_ref, zr1_ref, zm0_ref, zm1_ref, dsk_ref, nw_ref, o_ref, om_ref, s_ref) = refs
        z_refs = ((zr0_ref, zm0_ref), (zr1_ref, zm1_ref))
    else:
        (xs_ref, b_ref, c_ref, cs_ref, dtv_ref, cst_ref, dtt_ref, tot_ref, o_ref, s_ref) = refs
    step = pl.program_id(1)

    @pl.when(step == 0)
    def _():
        s_ref[...] = jnp.zeros_like(s_ref)

    ti = _iota((CHUNK, CHUNK), 0)
    si = _iota((CHUNK, CHUNK), 1)
    mask = (si > ti) if rev else (si <= ti)
    lane = _iota((CHUNK, LANE), 1)
    lo_half = lane < SSD_HEAD_DIM
    zero = jnp.zeros((), BF16)

    for grp in range(SSD_GROUPS):
        cols = slice(grp * SSD_GW, (grp + 1) * SSD_GW)
        xs = xs_ref[:, cols]
        bm = b_ref[:, grp * LANE:(grp + 1) * LANE]
        cm = c_ref[:, grp * LANE:(grp + 1) * LANE]
        a = cs_ref[grp]
        dt = dtv_ref[grp]
        a_t = cst_ref[grp]
        dt_t = dtt_ref[grp]
        tot = tot_ref[grp, 0:1, 0:LANE]
        decx = jnp.exp(tot_ref[grp, 0:1, LANE:LANE + SSD_GW])

        g = _nt(cm, bm)
        if rev:
            wst = dt * jnp.exp(a)
        else:
            wst = dt * jnp.exp(tot - a)
        cm32 = cm.astype(F32)
        state = s_ref[grp]
        sb = state.astype(BF16)
        pairs = []
        for p in range(SSD_HPG // 2):
            xp = xs[:, p * LANE:(p + 1) * LANE]
            sp = sb[:, p * LANE:(p + 1) * LANE]
            lhs, rhs = [], []
            for q in range(2):
                h = 2 * p + q
                acol = jnp.broadcast_to(a[:, h:h + 1], (CHUNK, CHUNK))
                arow = a_t[h:h + 1, :]
                e = (arow - acol) if rev else (acol - arow)
                wm = g * jnp.exp(jnp.where(mask, e, NEG)) * dt_t[h:h + 1, :]
                if rev:
                    t0 = LANE + SSD_GW + h * LANE
                    e1 = jnp.exp(tot_ref[grp, 0:1, t0:t0 + LANE] - acol)
                else:
                    e1 = jnp.exp(acol)
                mine = jnp.logical_not(lo_half) if q else lo_half
                lhs += [wm.astype(BF16), (cm32 * e1).astype(BF16)]
                rhs += [jnp.where(mine, xp, zero), jnp.where(mine, sp, zero)]
            pairs.append(_dot(jnp.concatenate(lhs, axis=1), jnp.concatenate(rhs, axis=0)))
        y = jnp.concatenate(pairs, axis=1)
        xw = (xs.astype(F32) * _widen_heads(wst)).astype(BF16)
        s_ref[grp] = decx * state + _tn(bm, xw)

        if final:
            zr_ref, zm_ref = z_refs[grp]
            z = jnp.where(step == 0, _meta_front(zm_ref[...], SSD_GW, BF16), zr_ref[...]).astype(F32)
            yt = y + yb_ref[:, cols] + xs.astype(F32) * dsk_ref[:, cols]
            yt = yt * _silu(z)
            ms = jnp.mean(yt * yt, axis=-1, keepdims=True)
            y = yt * lax.rsqrt(ms + EPS) * nw_ref[:, cols]
        o_ref[:, cols] = y.astype(o_ref.dtype)

    if final:
        @pl.when(step == 0)
        def _():
            om_ref[...] = o_ref[META_PAD:, :]


def ssd_scan(act, cs4, dt4, cst4, dtt4, tot4, grp, *, rev, ybwd=None, pr=None, pm=None,
             d_skip=None, norm_w=None):
    final = not rev
    nc1, bsz = grp.nc1, grp.bsz
    d = 1 if rev else 0
    cc = (lambda c: nc1 - 1 - c) if rev else (lambda c: c)
    bc_w = SSD_GROUPS * SSD_STATE
    b_blk = (COL_XBC + D_INNER) // bc_w
    ng = SSD_GROUPS
    in_specs = [
        pl.BlockSpec((None, CHUNK, D_INNER), lambda b, c: (b, cc(c), COL_XBC // D_INNER)),
        pl.BlockSpec((None, CHUNK, bc_w), lambda b, c: (b, cc(c), b_blk)),
        pl.BlockSpec((None, CHUNK, bc_w), lambda b, c: (b, cc(c), b_blk + 1)),
        pl.BlockSpec((None, ng, CHUNK, LANE), lambda b, c: (b, d, cc(c), 0)),
        pl.BlockSpec((None, ng, CHUNK, LANE), lambda b, c: (b, d, cc(c), 0)),
        pl.BlockSpec((None, ng, None, SSD_HPG, LANE), lambda b, c: (b, d, cc(c), 0, 0)),
        pl.BlockSpec((None, ng, None, SSD_HPG, LANE), lambda b, c: (b, d, cc(c), 0, 0)),
        pl.BlockSpec((None, ng, None, 8, SSD_TOT_W), lambda b, c: (b, d, cc(c), 0, 0)),
    ]
    args = [act, act, act, cs4, dt4, cst4, dtt4, tot4]
    if final:
        r128, mb = grp.rblk(CHUNK), grp.mblk()
        z_blk = COL_Z // SSD_GW

        def z_real(g):
            return pl.BlockSpec((CHUNK, SSD_GW), lambda b, c: (r128(b) + jnp.maximum(c - 1, 0), z_blk + g))

        def z_meta(g):
            return pl.BlockSpec((N_META, SSD_GW), lambda b, c: (mb(b), z_blk + g))

        in_specs += [
            pl.BlockSpec((None, CHUNK, D_INNER), lambda b, c: (b, c, 0)),
            z_real(0), z_real(1), z_meta(0), z_meta(1),
            pl.BlockSpec((1, D_INNER), lambda b, c: (0, 0)),
            pl.BlockSpec((1, D_INNER), lambda b, c: (0, 0)),
        ]
        args += [ybwd, pr, pr, pm, pm, d_skip, norm_w]
        out_specs = [pl.BlockSpec((CHUNK, D_INNER), lambda b, c: (r128(b) + jnp.maximum(c - 1, 0), 0)),
                     pl.BlockSpec((N_META, D_INNER), lambda b, c: (b, 0))]
        out_shape = [jax.ShapeDtypeStruct((bsz * grp.t, D_INNER), BF16),
                     jax.ShapeDtypeStruct((bsz * N_META, D_INNER), BF16)]
    else:
        out_specs = pl.BlockSpec((None, CHUNK, D_INNER), lambda b, c: (b, cc(c), 0))
        out_shape = jax.ShapeDtypeStruct((bsz, grp.lp, D_INNER), F32)
    return pl.pallas_call(
        functools.partial(_ssd_scan_kernel, rev=rev, final=final),
        grid=(bsz, nc1),
        in_specs=in_specs,
        out_specs=out_specs,
        out_shape=out_shape,
        scratch_shapes=[pltpu.VMEM((SSD_GROUPS, SSD_STATE, SSD_GW), F32)],
        compiler_params=_params(("parallel", "arbitrary")),
        name="ssd_scan_" + ("bwd" if rev else "fwd"),
    )(*args)


def _ret_scan_kernel(*refs, rev, final, nc1):
    if final:
        (dl_ref, qr_ref, qm_ref, kr_ref, km_ref, vr_ref, vm_ref, yb_ref, gr_ref, gm_ref,
         o_ref, om_ref, s_ref, dm_ref, e1_ref, ws_ref, dec_ref) = refs
    else:
        (dl_ref, qr_ref, qm_ref, kr_ref, km_ref, vr_ref, vm_ref,
         o_ref, s_ref, dm_ref, e1_ref, ws_ref, dec_ref) = refs
    step = pl.program_id(1)
    chunk = (nc1 - 1 - step) if rev else step
    is_meta = chunk == 0

    @pl.when(step == 0)
    def _():
        s_ref[...] = jnp.zeros_like(s_ref)
        ti = _iota((CHUNK, CHUNK), 0)
        si = _iota((CHUNK, CHUNK), 1)
        tf = ti.astype(F32)
        for h in range(RET_HEADS):
            x = jnp.full((CHUNK, CHUNK), dl_ref[1 if rev else 0, h], F32)
            lg = jnp.minimum(x, 0.0) - jnp.log1p(jnp.exp(-jnp.abs(x)))
            if rev:
                dm_ref[h] = jnp.exp(jnp.where(si > ti, (si - ti).astype(F32) * lg, NEG))
                e1_ref[h] = jnp.exp((CHUNK - tf) * lg)
                ws_ref[h] = jnp.exp(tf * lg)
            else:
                dm_ref[h] = jnp.exp(jnp.where(si <= ti, (ti - si).astype(F32) * lg, NEG))
                e1_ref[h] = jnp.exp((tf + 1.0) * lg)
                ws_ref[h] = jnp.exp((CHUNK - 1.0 - tf) * lg)
            dec_ref[h] = jnp.exp(CHUNK * lg)

    def pick(r_ref, m_ref):
        width = r_ref.shape[-1]
        return jnp.where(is_meta, _meta_front(m_ref[...], width, BF16), r_ref[...])

    q = pick(qr_ref, qm_ref)
    k = pick(kr_ref, km_ref)
    v = pick(vr_ref, vm_ref)
    if final:
        gate = pick(gr_ref, gm_ref)
    lane = _iota((CHUNK, LANE), 1)
    zero = jnp.zeros((), BF16)
    scale = jnp.asarray(RET_K_DIM ** -0.5, BF16)
    heads = range(RET_HEADS)
    pairs = [slice((h // 2) * LANE, (h // 2 + 1) * LANE) for h in heads]
    cols = [slice(h * LANE, (h + 1) * LANE) for h in heads]
    qh = [jnp.where((lane >= RET_K_DIM) if h % 2 else (lane < RET_K_DIM), q[:, pairs[h]], zero) * scale
          for h in heads]
    sc = [_nt(qh[h], k[:, pairs[h]]) for h in heads]
    inter = [_dot(qh[h], s_ref[h].astype(BF16)) for h in heads]
    for h in heads:
        vw = (v[:, cols[h]].astype(F32) * ws_ref[h]).astype(BF16)
        s_ref[h] = dec_ref[h] * s_ref[h] + _tn(k[:, pairs[h]], vw)
    ys = [_dot((sc[h] * dm_ref[h]).astype(BF16), v[:, cols[h]]) + e1_ref[h] * inter[h] for h in heads]
    for h in heads:
        y = ys[h]
        if final:
            yt = y + yb_ref[:, cols[h]]
            mu = jnp.mean(yt, axis=-1, keepdims=True)
            dev = yt - mu
            var = jnp.mean(dev * dev, axis=-1, keepdims=True)
            y = dev * lax.rsqrt(var + 1e-5) * _silu(gate[:, cols[h]].astype(F32))
        o_ref[:, cols[h]] = y.astype(o_ref.dtype)

    if final:
        @pl.when(step == 0)
        def _():
            om_ref[...] = o_ref[META_PAD:, :]


def ret_scan(pr, pm, decay, grp, *, rev, ybwd=None):
    final = not rev
    nc1, bsz = grp.nc1, grp.bsz
    r128, mb = grp.rblk(CHUNK), grp.mblk()
    cc = (lambda c: nc1 - 1 - c) if rev else (lambda c: c)
    qk_w = RET_HEADS * RET_K_DIM

    def real(col0, width):
        blk = col0 // width
        return pl.BlockSpec((CHUNK, width), lambda b, c: (r128(b) + jnp.maximum(cc(c) - 1, 0), blk))

    def meta(col0, width):
        blk = col0 // width
        return pl.BlockSpec((N_META, width), lambda b, c: (mb(b), blk))

    in_specs = [pl.BlockSpec(memory_space=pltpu.SMEM),
                real(COL_RET_Q, qk_w), meta(COL_RET_Q, qk_w),
                real(COL_RET_K, qk_w), meta(COL_RET_K, qk_w),
                real(COL_RET_V, GROUP_WIDTH), meta(COL_RET_V, GROUP_WIDTH)]
    args = [decay, pr, pm, pr, pm, pr, pm]
    if final:
        in_specs += [pl.BlockSpec((None, CHUNK, GROUP_WIDTH), lambda b, c: (b, c, 0)),
                     real(COL_RET_G, GROUP_WIDTH), meta(COL_RET_G, GROUP_WIDTH)]
        args += [ybwd, pr, pm]
        out_specs = [pl.BlockSpec((CHUNK, GROUP_WIDTH), lambda b, c: (r128(b) + jnp.maximum(c - 1, 0), 0)),
                     pl.BlockSpec((N_META, GROUP_WIDTH), lambda b, c: (b, 0))]
        out_shape = [jax.ShapeDtypeStruct((bsz * grp.t, GROUP_WIDTH), BF16),
                     jax.ShapeDtypeStruct((bsz * N_META, GROUP_WIDTH), BF16)]
    else:
        out_specs = pl.BlockSpec((None, CHUNK, GROUP_WIDTH), lambda b, c: (b, cc(c), 0))
        out_shape = jax.ShapeDtypeStruct((bsz, grp.lp, GROUP_WIDTH), F32)
    table = pltpu.VMEM((RET_HEADS, CHUNK, CHUNK), F32)
    return pl.pallas_call(
        functools.partial(_ret_scan_kernel, rev=rev, final=final, nc1=nc1),
        grid=(bsz, nc1),
        in_specs=in_specs,
        out_specs=out_specs,
        out_shape=out_shape,
        scratch_shapes=[pltpu.VMEM((RET_HEADS, LANE, RET_V_DIM), F32), table, table, table, table],
        compiler_params=_params(("parallel", "arbitrary")),
        name="ret_scan_" + ("bwd" if rev else "fwd"),
    )(*args)


def _alibi_slope_rows(rows_per_head, nrows, g):
    hd = _iota((nrows, 1), 0) // rows_per_head
    s = jnp.full((nrows, 1), 2.0 ** -SWA_REP, F32)
    for r in range(SWA_REP - 1):
        s = jnp.where(hd == r, 2.0 ** -(r + 1), s)
    return s * jnp.where(g == 1, 2.0 ** -SWA_REP, 1.0)


def _sink_rows(sink_ref, rows_per_head, nrows, g):
    hd = _iota((nrows, 1), 0) // rows_per_head
    s = jnp.full((nrows, 1), sink_ref[g * SWA_REP + SWA_REP - 1], F32)
    for r in range(SWA_REP - 1):
        s = jnp.where(hd == r, sink_ref[g * SWA_REP + r], s)
    return s


def _softmax_with_sink(s, sink, vall):
    m = jnp.maximum(jnp.max(s, axis=-1, keepdims=True), sink)
    p = jnp.exp(s - m)
    denom = jnp.sum(p, axis=-1, keepdims=True) + jnp.exp(sink - m)
    return _dot(p.astype(BF16), vall) / denom


def _swa_kernel(sink_ref, q0_ref, q1_ref, kp_ref, ko_ref, kn_ref, vp_ref, vo_ref, vn_ref, km_ref, vm_ref,
                o_ref, bias_ref, s_ref, p_ref, l_ref, *, nb):
    q_refs = (q0_ref, q1_ref)
    n = pl.program_id(1)
    blk = SWA_BLOCK
    nq = SWA_REP * blk
    nkeys = 4 * blk
    groups = range(SWA_KV_HEADS)

    @pl.when(n == 0)
    def _():
        t = _iota((nq, nkeys), 0) % blk
        col = _iota((nq, nkeys), 1)
        rel = col - blk - t
        dist = jnp.abs(rel)
        band = jnp.logical_and(col < 3 * blk, dist <= SWA_WINDOW)
        is_meta = jnp.logical_and(col >= 3 * blk, col < 3 * blk + N_META)
        for g in groups:
            slope = _alibi_slope_rows(blk, nq, g)
            bias_ref[g] = jnp.where(band, -slope * dist.astype(F32), jnp.where(is_meta, 0.0, NEG))

    pad = jnp.zeros((META_PAD, SWA_KV_HEADS * SWA_HEAD_DIM), BF16)
    kall = jnp.concatenate([kp_ref[...], ko_ref[...], kn_ref[...], km_ref[...], pad], axis=0)
    vall = jnp.concatenate([vp_ref[...], vo_ref[...], vn_ref[...], vm_ref[...], pad], axis=0)
    for g, q_ref in enumerate(q_refs):
        q = q_ref[...]
        qs = jnp.concatenate([q[:, r * blk:(r + 1) * blk] for r in range(SWA_REP)], axis=0)
        s_ref[g] = _nt(qs, kall[:, g * LANE:(g + 1) * LANE])
    col = _iota((1, nkeys), 1)
    lo = jnp.where(n == 0, blk, 0)
    hi = jnp.where(n == nb - 1, 2 * blk, 3 * blk)
    outside = jnp.logical_or(col < lo, jnp.logical_and(col >= hi, col < 3 * blk))
    scale = SWA_HEAD_DIM ** -0.5
    outs = []
    for g in groups:
        sink = _sink_rows(sink_ref, blk, nq, g)
        for c in range(nq // SWA_SM_ROWS):
            rows = slice(c * SWA_SM_ROWS, (c + 1) * SWA_SM_ROWS)
            s = jnp.where(outside, NEG, s_ref[g, rows, :] * scale + bias_ref[g, rows, :])
            sk = sink[rows]
            m = jnp.maximum(jnp.max(s, axis=-1, keepdims=True), sk)
            p = jnp.exp(s - m)
            p_ref[g, rows, :] = p.astype(BF16)
            denom = jnp.sum(p, axis=-1, keepdims=True) + jnp.exp(sk - m)
            l_ref[g, rows, :] = jnp.broadcast_to(1.0 / denom, (SWA_SM_ROWS, LANE))
        o = _dot(p_ref[g], vall[:, g * LANE:(g + 1) * LANE]) * l_ref[g]
        outs += [o[r * blk:(r + 1) * blk] for r in range(SWA_REP)]
    o_ref[...] = jnp.concatenate(outs, axis=1).astype(o_ref.dtype)


def swa_real(pr, pm, sink, grp):
    bsz, nb = grp.bsz, grp.t // SWA_BLOCK
    r128, mb = grp.rblk(SWA_BLOCK), grp.mblk()
    kvw = SWA_KV_HEADS * SWA_HEAD_DIM
    qw = SWA_REP * SWA_HEAD_DIM
    q_blk, k_blk, v_blk = COL_SWA_Q // qw, COL_SWA_K // kvw, COL_SWA_V // kvw
    nq, nkeys = SWA_REP * SWA_BLOCK, 4 * SWA_BLOCK

    def kv(col_blk, off):
        return pl.BlockSpec((SWA_BLOCK, kvw),
                            lambda b, n: (r128(b) + jnp.clip(n + off, 0, nb - 1), col_blk))

    in_specs = [pl.BlockSpec(memory_space=pltpu.SMEM),
                pl.BlockSpec((SWA_BLOCK, qw), lambda b, n: (r128(b) + n, q_blk)),
                pl.BlockSpec((SWA_BLOCK, qw), lambda b, n: (r128(b) + n, q_blk + 1)),
                kv(k_blk, -1), kv(k_blk, 0), kv(k_blk, 1),
                kv(v_blk, -1), kv(v_blk, 0), kv(v_blk, 1),
                pl.BlockSpec((N_META, kvw), lambda b, n: (mb(b), k_blk)),
                pl.BlockSpec((N_META, kvw), lambda b, n: (mb(b), v_blk))]
    return pl.pallas_call(
        functools.partial(_swa_kernel, nb=nb),
        grid=(bsz, nb),
        in_specs=in_specs,
        out_specs=pl.BlockSpec((SWA_BLOCK, GROUP_WIDTH), lambda b, n: (b * nb + n, 0)),
        out_shape=jax.ShapeDtypeStruct((bsz * grp.t, GROUP_WIDTH), BF16),
        scratch_shapes=[pltpu.VMEM((SWA_KV_HEADS, nq, nkeys), F32),
                        pltpu.VMEM((SWA_KV_HEADS, nq, nkeys), F32),
                        pltpu.VMEM((SWA_KV_HEADS, nq, nkeys), BF16),
                        pltpu.VMEM((SWA_KV_HEADS, nq, LANE), F32)],
        compiler_params=_params(("parallel", "arbitrary")),
        name="swa_real",
    )(sink, pr, pr, pr, pr, pr, pr, pr, pr, pm, pm)


def _swa_meta_kernel(sink_ref, q_ref, k0_ref, v0_ref, km_ref, vm_ref, o_ref):
    g = pl.program_id(1)
    nq = SWA_REP * N_META
    nkeys = 2 * SWA_BLOCK
    q = q_ref[...]
    qs = jnp.concatenate([q[:, r * LANE:(r + 1) * LANE] for r in range(SWA_REP)], axis=0)
    pad = jnp.zeros((META_PAD, SWA_HEAD_DIM), BF16)
    kall = jnp.concatenate([k0_ref[...], km_ref[...], pad], axis=0)
    vall = jnp.concatenate([v0_ref[...], vm_ref[...], pad], axis=0)
    i = _iota((nq, nkeys), 0) % N_META
    col = _iota((nq, nkeys), 1)
    mdist = N_META + col - i
    near = jnp.logical_and(col < SWA_BLOCK, mdist <= SWA_WINDOW)
    is_meta = jnp.logical_and(col >= SWA_BLOCK, col < SWA_BLOCK + N_META)
    slope = _alibi_slope_rows(N_META, nq, g)
    bias = jnp.where(near, -slope * mdist.astype(F32), jnp.where(is_meta, 0.0, NEG))
    s = _nt(qs, kall) * (SWA_HEAD_DIM ** -0.5) + bias
    o = _softmax_with_sink(s, _sink_rows(sink_ref, N_META, nq, g), vall)
    o_ref[...] = jnp.concatenate([o[r * N_META:(r + 1) * N_META] for r in range(SWA_REP)],
                                 axis=1).astype(o_ref.dtype)


def swa_meta(pr, pm, sink, grp):
    bsz = grp.bsz
    r128, mb = grp.rblk(SWA_BLOCK), grp.mblk()
    qw = SWA_REP * SWA_HEAD_DIM
    q_blk, k_blk, v_blk = COL_SWA_Q // qw, COL_SWA_K // LANE, COL_SWA_V // LANE
    in_specs = [pl.BlockSpec(memory_space=pltpu.SMEM),
                pl.BlockSpec((N_META, qw), lambda b, g: (mb(b), q_blk + g)),
                pl.BlockSpec((SWA_BLOCK, LANE), lambda b, g: (r128(b), k_blk + g)),
                pl.BlockSpec((SWA_BLOCK, LANE), lambda b, g: (r128(b), v_blk + g)),
                pl.BlockSpec((N_META, LANE), lambda b, g: (mb(b), k_blk + g)),
                pl.BlockSpec((N_META, LANE), lambda b, g: (mb(b), v_blk + g))]
    return pl.pallas_call(
        _swa_meta_kernel,
        grid=(bsz, SWA_KV_HEADS),
        in_specs=in_specs,
        out_specs=pl.BlockSpec((N_META, qw), lambda b, g: (b, g)),
        out_shape=jax.ShapeDtypeStruct((bsz * N_META, GROUP_WIDTH), BF16),
        compiler_params=_params(("parallel", "parallel")),
        name="swa_meta",
    )(sink, pm, pr, pr, pm, pm)


NA_KROWS = 3 * NA_QROWS
NA_RPB_H = 2 * NA_KH - 1
NA_RPB_W = 2 * NA_KW - 1
NA_VARIANTS = 3
NA_SM_ROWS = 32
NA_HPS = 8


def _na_row_ok(variant, i, j):
    if variant == 0:
        return NA_QROWS <= j < NA_QROWS + NA_KH
    if variant == 1:
        return i <= j < i + NA_KH
    return j < NA_KH


def _na_bias_kernel(rpb_ref, o_ref):
    h = pl.program_id(0)
    base = h * (NA_RPB_H * NA_RPB_W)
    qc = _iota((GRID_W, LANE), 0)
    lane = _iota((GRID_W, LANE), 1)
    lo_half = lane < GRID_W
    kc = jnp.where(lo_half, lane, lane - GRID_W)
    d = kc - qc + (NA_KW - 1)
    col_start = jnp.clip(qc - NA_KW // 2, 0, GRID_W - NA_KW)
    col_ok = jnp.logical_and(kc >= col_start, kc < col_start + NA_KW)
    neg = jnp.full((GRID_W, LANE), NEG, F32)
    for a in range(NA_RPB_H - 1):
        val = jnp.zeros((GRID_W, LANE), F32)
        for b in range(NA_RPB_W):
            s_lo = rpb_ref[base + a * NA_RPB_W + b]
            s_hi = rpb_ref[base + (a + 1) * NA_RPB_W + b]
            val = jnp.where(d == b, jnp.where(lo_half, s_lo, s_hi), val)
        tile = jnp.where(col_ok, val, neg)
        for jp in range(NA_KROWS // 2):
            i = 2 * jp + (NA_QROWS - 1) - a
            if not 0 <= i < NA_QROWS:
                continue
            for v in range(NA_VARIANTS):
                ok_lo, ok_hi = _na_row_ok(v, i, 2 * jp), _na_row_ok(v, i, 2 * jp + 1)
                if ok_lo and ok_hi:
                    blk = tile
                elif ok_lo:
                    blk = jnp.where(lo_half, tile, neg)
                elif ok_hi:
                    blk = jnp.where(lo_half, neg, tile)
                else:
                    blk = neg
                o_ref[v, i * GRID_W:(i + 1) * GRID_W, jp * LANE:(jp + 1) * LANE] = blk
    meta_cols = jnp.where(_iota((NA_QB, LANE), 1) < N_META, 0.0, NEG)
    for v in range(NA_VARIANTS):
        o_ref[v, :, 3 * NA_QB:] = meta_cols


def na_bias(rpb):
    return pl.pallas_call(
        _na_bias_kernel,
        grid=(NA_HEADS,),
        in_specs=[pl.BlockSpec(memory_space=pltpu.SMEM)],
        out_specs=pl.BlockSpec((NA_VARIANTS, None, NA_QB, NA_KEYS), lambda h: (0, h, 0, 0)),
        out_shape=jax.ShapeDtypeStruct((NA_VARIANTS, NA_HEADS, NA_QB, NA_KEYS), F32),
        compiler_params=_params(("parallel",)),
        name="na_bias",
    )(rpb.astype(F32).reshape(-1))


def _na_kernel(q_ref, kp_ref, ko_ref, kn_ref, vp_ref, vo_ref, vn_ref, km_ref, vm_ref,
               bias_ref, o_ref, s_ref, p_ref, l_ref):
    q = q_ref[...] * jnp.asarray(NA_HEAD_DIM ** -0.5, BF16)
    width = NA_HPS * NA_HEAD_DIM
    pad = jnp.zeros((META_PAD, width), BF16)
    kall = jnp.concatenate([kp_ref[...], ko_ref[...], kn_ref[...], km_ref[...], pad], axis=0)
    vall = jnp.concatenate([vp_ref[...], vo_ref[...], vn_ref[...], vm_ref[...], pad], axis=0)
    zero = jnp.zeros((), BF16)
    qlane = _iota((NA_QB, LANE), 1)
    vlane = _iota((NA_KEYS, LANE), 1)
    tiles = [slice((hh // 2) * LANE, (hh // 2 + 1) * LANE) for hh in range(NA_HPS)]
    for hh in range(NA_HPS):
        qsel = (qlane >= NA_HEAD_DIM) if hh % 2 else (qlane < NA_HEAD_DIM)
        s_ref[hh] = _nt(jnp.where(qsel, q[:, tiles[hh]], zero), kall[:, tiles[hh]])
    outs = [None] * (NA_HPS // 2)
    for hh in range(NA_HPS):
        for c in range(NA_QB // NA_SM_ROWS):
            rows = slice(c * NA_SM_ROWS, (c + 1) * NA_SM_ROWS)
            s = s_ref[hh, rows, :] + bias_ref[hh, rows, :]
            m = jnp.max(s, axis=-1, keepdims=True)
            p = jnp.exp(s - m)
            p_ref[hh, rows, :] = p.astype(BF16)
            l_ref[hh, rows, :] = jnp.broadcast_to(1.0 / jnp.sum(p, axis=-1, keepdims=True),
                                                  (NA_SM_ROWS, LANE))
        vsel = (vlane >= NA_HEAD_DIM) if hh % 2 else (vlane < NA_HEAD_DIM)
        o = _dot(p_ref[hh], jnp.where(vsel, vall[:, tiles[hh]], zero)) * l_ref[hh]
        outs[hh // 2] = o if outs[hh // 2] is None else outs[hh // 2] + o
    o_ref[...] = jnp.concatenate(outs, axis=1).astype(o_ref.dtype)


def na_real(pr, pm, bias, grp):
    bsz, nqb = grp.bsz, grp.t // NA_QB
    rq, mb = grp.rblk(NA_QB), grp.mblk()
    width = NA_HPS * NA_HEAD_DIM
    q_blk, k_blk, v_blk = COL_NA_Q // width, COL_NA_K // width, COL_NA_V // width

    def kv(col_blk, off):
        return pl.BlockSpec((NA_QB, width),
                            lambda p, b, n: (rq(b) + jnp.clip(n + off, 0, nqb - 1), col_blk + p))

    in_specs = [pl.BlockSpec((NA_QB, width), lambda p, b, n: (rq(b) + n, q_blk + p)),
                kv(k_blk, -1), kv(k_blk, 0), kv(k_blk, 1),
                kv(v_blk, -1), kv(v_blk, 0), kv(v_blk, 1),
                pl.BlockSpec((N_META, width), lambda p, b, n: (mb(b), k_blk + p)),
                pl.BlockSpec((N_META, width), lambda p, b, n: (mb(b), v_blk + p)),
                pl.BlockSpec((None, NA_HPS, NA_QB, NA_KEYS),
                             lambda p, b, n: (jnp.where(n == 0, 0, jnp.where(n == nqb - 1, 2, 1)), p, 0, 0))]
    return pl.pallas_call(
        _na_kernel,
        grid=(NA_HEADS // NA_HPS, bsz, nqb),
        in_specs=in_specs,
        out_specs=pl.BlockSpec((NA_QB, width), lambda p, b, n: (b * nqb + n, p)),
        out_shape=jax.ShapeDtypeStruct((bsz * grp.t, GROUP_WIDTH), BF16),
        scratch_shapes=[pltpu.VMEM((NA_HPS, NA_QB, NA_KEYS), F32),
                        pltpu.VMEM((NA_HPS, NA_QB, NA_KEYS), BF16),
                        pltpu.VMEM((NA_HPS, NA_QB, LANE), F32)],
        compiler_params=_params(("parallel", "parallel", "arbitrary")),
        name="na_real",
    )(pr, pr, pr, pr, pr, pr, pr, pm, pm, bias)


def _na_meta_kernel(q_ref, k_ref, v_ref, km_ref, vm_ref, o_ref):
    nwin = NA_KH * GRID_W
    nkeys = nwin + CHUNK
    q = q_ref[...]
    pad = jnp.zeros((META_PAD, LANE), BF16)
    kall = jnp.concatenate([k_ref[...], km_ref[...], pad], axis=0)
    vall = jnp.concatenate([v_ref[...], vm_ref[...], pad], axis=0)
    col = _iota((N_META, nkeys), 1)
    in_win = jnp.logical_and(col < nwin, col % GRID_W < NA_KW)
    is_meta = jnp.logical_and(col >= nwin, col < nwin + N_META)
    bias = jnp.where(jnp.logical_or(in_win, is_meta), 0.0, NEG)
    zero = jnp.zeros((), BF16)
    qlane = _iota((N_META, LANE), 1)
    vlane = _iota((nkeys, LANE), 1)
    out = jnp.zeros((N_META, LANE), F32)
    for hh in range(2):
        qsel = (qlane >= NA_HEAD_DIM) if hh else (qlane < NA_HEAD_DIM)
        vsel = (vlane >= NA_HEAD_DIM) if hh else (vlane < NA_HEAD_DIM)
        s = _nt(jnp.where(qsel, q, zero), kall) * (NA_HEAD_DIM ** -0.5) + bias
        m = jnp.max(s, axis=-1, keepdims=True)
        p = jnp.exp(s - m)
        denom = jnp.sum(p, axis=-1, keepdims=True)
        out = out + _dot(p.astype(BF16), jnp.where(vsel, vall, zero)) / denom
    o_ref[...] = out.astype(o_ref.dtype)


def na_meta(pr, pm, grp):
    bsz = grp.bsz
    nwin = NA_KH * GRID_W
    rw, mb = grp.rblk(nwin), grp.mblk()
    q_blk, k_blk, v_blk = COL_NA_Q // LANE, COL_NA_K // LANE, COL_NA_V // LANE
    in_specs = [pl.BlockSpec((N_META, LANE), lambda b, p: (mb(b), q_blk + p)),
                pl.BlockSpec((nwin, LANE), lambda b, p: (rw(b), k_blk + p)),
                pl.BlockSpec((nwin, LANE), lambda b, p: (rw(b), v_blk + p)),
                pl.BlockSpec((N_META, LANE), lambda b, p: (mb(b), k_blk + p)),
                pl.BlockSpec((N_META, LANE), lambda b, p: (mb(b), v_blk + p))]
    return pl.pallas_call(
        _na_meta_kernel,
        grid=(bsz, NA_HEADS // 2),
        in_specs=in_specs,
        out_specs=pl.BlockSpec((N_META, LANE), lambda b, p: (b, p)),
        out_shape=jax.ShapeDtypeStruct((bsz * N_META, GROUP_WIDTH), BF16),
        compiler_params=_params(("parallel", "parallel")),
        name="na_meta",
    )(pm, pr, pr, pm, pm)


def mix_group(pr, pm, dtr, dtm, lw, grp, need_meta):
    act, cs4, dt4, cst4, dtt4, tot4 = ssd_pre(pr, pm, dtr, dtm, lw["conv_w"], lw["conv_b"],
                                               lw["bias4"], lw["alog4"], grp)
    yb = ssd_scan(act, cs4, dt4, cst4, dtt4, tot4, grp, rev=True)
    y_ssd, ym_ssd = ssd_scan(act, cs4, dt4, cst4, dtt4, tot4, grp, rev=False, ybwd=yb, pr=pr, pm=pm,
                             d_skip=lw["d_skip"], norm_w=lw["ssd_norm_w"])
    rb = ret_scan(pr, pm, lw["ret_decay"], grp, rev=True)
    y_ret, ym_ret = ret_scan(pr, pm, lw["ret_decay"], grp, rev=False, ybwd=rb)
    real = [y_ssd, swa_real(pr, pm, lw["sink"], grp), na_real(pr, pm, lw["na_bias"], grp), y_ret]
    meta = None
    if need_meta:
        meta = [ym_ssd, swa_meta(pr, pm, lw["sink"], grp), na_meta(pr, pm, grp), ym_ret]
    return real, meta


def _layer_weights(i, w_in, ssd_conv_w, ssd_conv_b, ssd_dt_bias, ssd_a_log, ssd_d, ssd_norm_w,
                   swa_sink, na_rpb, ret_decay, w_out, w_up, w_down):
    w_main, w_dt = cast_w_in(w_in, i)

    def per_dir_group(v):
        v4 = v.astype(F32).reshape(2 * SSD_GROUPS, 1, SSD_HPG)
        return jnp.pad(v4, ((0, 0), (0, 0), (0, LANE - SSD_HPG)))

    return dict(
        w_main=w_main, w_dt=w_dt,
        conv_w=jnp.pad(ssd_conv_w[i].astype(F32), ((0, 8 - SSD_CONV_W), (0, 0))),
        conv_b=ssd_conv_b[i].astype(F32).reshape(1, SSD_CONV_DIM),
        bias4=per_dir_group(ssd_dt_bias[i]), alog4=per_dir_group(ssd_a_log[i]),
        d_skip=jnp.repeat(ssd_d[i].astype(F32), SSD_HEAD_DIM).reshape(1, D_INNER),
        ssd_norm_w=ssd_norm_w[i].astype(F32).reshape(1, D_INNER),
        sink=swa_sink[i].astype(F32), na_bias=na_bias(na_rpb[i]),
        ret_decay=ret_decay[i].astype(F32),
        w_out=cast_bf16(w_out, i), w_up=cast_bf16(w_up, i), w_down=cast_bf16(w_down, i),
    )


TM_REAL = 1024
TN = 1024
TN_BF16 = 1024
TK_DOWN = 4096
TM_NORM = 256


def _mlp(x, tm, lw, norm_w):
    n2 = rmsnorm(x, norm_w, BF16, min(tm, TM_NORM))
    u = matmul(n2, lw["w_up"], tm=tm, tn=TN_BF16, tk=D_MODEL, out_dtype=BF16, epilogue="relu2")
    return matmul(u, lw["w_down"], tm=tm, tn=TN, tk=TK_DOWN, out_dtype=F32,
                  epilogue="residual", residual=x)


def kernel(x_prompt, x_sample, meta_tokens, norm1_w, w_in, ssd_conv_w, ssd_conv_b, ssd_dt_bias,
           ssd_a_log, ssd_d, ssd_norm_w, swa_sink, na_rpb, ret_decay, w_out, norm2_w, w_up, w_down,
           final_norm_w):
    d = D_MODEL
    inputs = [x_prompt, x_sample]
    groups, mbase = [], 0
    for x in inputs:
        groups.append(Group(mbase, x.shape[0], x.shape[1]))
        mbase += x.shape[0] * N_META
    n_meta = mbase
    xs = [x.reshape(-1, d).astype(F32) for x in inputs]
    xm = jnp.tile(meta_tokens.astype(F32), (n_meta // N_META, 1))

    for i in range(DEPTH):
        last = i == DEPTH - 1
        lw = _layer_weights(i, w_in, ssd_conv_w, ssd_conv_b, ssd_dt_bias, ssd_a_log, ssd_d,
                            ssd_norm_w, swa_sink, na_rpb, ret_decay, w_out, w_up, w_down)
        nm = rmsnorm(xm, norm1_w[i], BF16, n_meta)
        pm = matmul(nm, lw["w_main"], tm=n_meta, tn=TN_BF16, tk=d, out_dtype=BF16)
        dtm = matmul(nm, lw["w_dt"], tm=n_meta, tn=LANE, tk=d, out_dtype=F32)
        meta_parts = []
        for gi, grp in enumerate(groups):
            nr = rmsnorm(xs[gi], norm1_w[i], BF16, TM_NORM)
            pr = matmul(nr, lw["w_main"], tm=TM_REAL, tn=TN_BF16, tk=d, out_dtype=BF16)
            dtr = matmul(nr, lw["w_dt"], tm=TM_REAL, tn=LANE, tk=d, out_dtype=F32)
            real, meta = mix_group(pr, pm, dtr, dtm, lw, grp, need_meta=not last)
            meta_parts.append(meta)
            h = matmul(real, lw["w_out"], tm=TM_REAL, tn=TN, tk=GROUP_WIDTH, out_dtype=F32,
                       epilogue="residual", residual=xs[gi], alias=i > 0)
            xs[gi] = _mlp(h, TM_REAL, lw, norm2_w[i])
        if not last:
            mixed_m = [jnp.concatenate([m[k] for m in meta_parts], axis=0) for k in range(4)]
            hm = matmul(mixed_m, lw["w_out"], tm=n_meta, tn=TN, tk=GROUP_WIDTH, out_dtype=F32,
                        epilogue="residual", residual=xm)
            xm = _mlp(hm, n_meta, lw, norm2_w[i])

    outs = [rmsnorm(x, final_norm_w, F32, TM_NORM).reshape(inp.shape)
            for x, inp in zip(xs, inputs)]
    return tuple(outs)
```
